```python
import jax, jax.numpy as jnp
from jax import lax
import numpy as np

D_MODEL = 1024
BATCH = 32
SEQ = 256
DEPTH = 1
DEC_BATCH = 8
DEC_SEQ = 4096
PAST_LEN = 512

GRID_W = 64
N_FOUR_GROUPS = 4
FOUR_GROUP = 128
FOUR_W = N_FOUR_GROUPS * FOUR_GROUP
N_HEADS = 4
HEAD_K = 128
HEAD_V = 128
HGRN_K_W = N_HEADS * HEAD_K
HGRN_V_W = N_HEADS * HEAD_V
D_FF = 2816
CHUNK = 32
N_MOD = 9
EPS = 1e-6
IN_COLS = FOUR_W + 3 * HGRN_K_W + 2 * HGRN_V_W + 2 * D_MODEL

kernel_name = "hybrid_fnet_hgrn2_macaron_diffusion_step"


def rms_norm(x, gain):
    xf = x.astype(jnp.float32)
    y = xf * lax.rsqrt(jnp.mean(xf * xf, axis=-1, keepdims=True) + EPS)
    return (y * gain.astype(jnp.float32)).astype(x.dtype)


def swiglu(h, w_in, w_out):
    g, u = jnp.split(h @ w_in, 2, axis=-1)
    return (jax.nn.silu(g) * u) @ w_out


def fourier_mix(u, rows):
    B, L, _ = u.shape
    uf = u.astype(jnp.float32)
    if rows is None:
        uf = uf.reshape(B, L, N_FOUR_GROUPS, FOUR_GROUP)
        y = jnp.fft.fftn(uf, axes=(1, 3), norm="ortho").real
    else:
        uf = uf.reshape(B, rows, GRID_W, N_FOUR_GROUPS, FOUR_GROUP)
        y = jnp.fft.fftn(uf, axes=(1, 2, 4), norm="ortho").real
    return y.reshape(B, L, FOUR_W).astype(u.dtype)


def _to_chunks(a):
    B, L, H, E = a.shape
    return a.reshape(B, L // CHUNK, CHUNK, H, E).transpose(1, 0, 3, 2, 4)


def hgrn2_scan(q, f, v, s0):
    B, L, H, _ = q.shape
    log_f = jnp.log(f)
    k = 1.0 - f
    tril = jnp.tril(jnp.ones((CHUNK, CHUNK), dtype=bool))[:, :, None]

    def step(S, inp):
        qc, lfc, kc, vc = inp
        b = jnp.cumsum(lfc, axis=2)
        diff = b[:, :, :, None, :] - b[:, :, None, :, :]
        decay = jnp.where(tril, jnp.exp(jnp.where(tril, diff, 0.0)), 0.0)
        A = jnp.sum(qc[:, :, :, None, :] * kc[:, :, None, :, :] * decay, axis=-1)
        o = (jnp.einsum('bhts,bhsv->bhtv', A, vc)
             + jnp.einsum('bhtk,bhkv->bhtv', qc * jnp.exp(b), S))
        b_last = b[:, :, -1:, :]
        S = (jnp.exp(b_last[:, :, 0, :])[..., None] * S
             + jnp.einsum('bhsk,bhsv->bhkv', kc * jnp.exp(b_last - b), vc))
        return S, o

    s_final, o = lax.scan(step, s0, (_to_chunks(q), _to_chunks(log_f), _to_chunks(k), _to_chunks(v)))
    o = o.transpose(1, 0, 3, 2, 4).reshape(B, L, H, v.shape[-1])
    return o, s_final


def mixer(h, w_in, w_four, hgrn_gain, w_hgrn, w_out, lb, s0_f, s0_b, rows):
    B, L, _ = h.shape
    splits = [int(s) for s in np.cumsum([FOUR_W, HGRN_K_W, HGRN_K_W, HGRN_K_W, HGRN_V_W, HGRN_V_W, D_MODEL])]
    u, q, zf, zb, i, g, ga, gb = jnp.split(h @ w_in, splits, axis=-1)
    branch_a = fourier_mix(u, rows) @ w_four
    qh = jax.nn.silu(q.astype(jnp.float32)).reshape(B, L, N_HEADS, HEAD_K)
    vh = i.astype(jnp.float32).reshape(B, L, N_HEADS, HEAD_V)
    f_f = (lb[0] + (1.0 - lb[0]) * jax.nn.sigmoid(zf.astype(jnp.float32))).reshape(B, L, N_HEADS, HEAD_K)
    f_b = (lb[1] + (1.0 - lb[1]) * jax.nn.sigmoid(zb.astype(jnp.float32))).reshape(B, L, N_HEADS, HEAD_K)
    o_f, s_f = hgrn2_scan(qh, f_f, vh, s0_f)
    o_b, s_b = hgrn2_scan(jnp.flip(qh, 1), jnp.flip(f_b, 1), jnp.flip(vh, 1), s0_b)
    o = o_f + jnp.flip(o_b, 1)
    o = o * lax.rsqrt(jnp.mean(o * o, axis=-1, keepdims=True) + EPS)
    o = o.reshape(B, L, HGRN_V_W) * hgrn_gain.astype(jnp.float32) * jax.nn.silu(g.astype(jnp.float32))
    branch_b = o.astype(h.dtype) @ w_hgrn
    merged = jax.nn.sigmoid(ga) * branch_a + jax.nn.sigmoid(gb) * branch_b
    return merged @ w_out, s_f, s_b


def trunk_layer(x, cond, w_mod, b_mod, norm_pre, norm_post, ffn_w_in, ffn_w_out,
                w_in, w_four, hgrn_gain, w_hgrn, w_out, lb, s0_f, s0_b, rows):
    mod = (jax.nn.silu(cond) @ w_mod + b_mod)[:, None, :]
    sh1, sc1, g1, sh2, sc2, g2, sh3, sc3, g3 = jnp.split(mod, N_MOD, axis=-1)
    h = rms_norm(x, norm_pre[0]) * (1.0 + sc1) + sh1
    x = x + 0.5 * g1 * rms_norm(swiglu(h, ffn_w_in[0], ffn_w_out[0]), norm_post[0])
    h = rms_norm(x, norm_pre[1]) * (1.0 + sc2) + sh2
    m, s_f, s_b = mixer(h, w_in, w_four, hgrn_gain, w_hgrn, w_out, lb, s0_f, s0_b, rows)
    x = x + g2 * rms_norm(m, norm_post[1])
    h = rms_norm(x, norm_pre[2]) * (1.0 + sc3) + sh3
    x = x + 0.5 * g3 * rms_norm(swiglu(h, ffn_w_in[1], ffn_w_out[1]), norm_post[2])
    return x, s_f, s_b


def setup_inputs(seed: int = 0) -> dict:
    key = jax.random.key(seed)
    ks = jax.random.split(key, 17)
    n = jax.random.normal
    f32 = jnp.float32
    return {
        "x_prompt": n(ks[0], (BATCH, SEQ, D_MODEL), f32),
        "x_sample": n(ks[1], (DEC_BATCH, DEC_SEQ, D_MODEL), f32),
        "state_hgrn": 0.5 * n(ks[2], (DEC_BATCH, DEPTH, 2, N_HEADS, HEAD_K, HEAD_V), f32),
        "c": n(ks[3], (DEC_BATCH, D_MODEL), f32),
        "c_ctx": n(ks[4], (D_MODEL,), f32),
        "w_mod": 0.5 * D_MODEL ** -0.5 * n(ks[5], (DEPTH, D_MODEL, N_MOD * D_MODEL), f32),
        "b_mod": 0.01 * n(ks[6], (DEPTH, N_MOD * D_MODEL), f32),
        "norm_pre": 1.0 + 0.05 * n(ks[7], (DEPTH, 3, D_MODEL), f32),
        "norm_post": 1.0 + 0.05 * n(ks[8], (DEPTH, 3, D_MODEL), f32),
        "ffn_w_in": D_MODEL ** -0.5 * n(ks[9], (DEPTH, 2, D_MODEL, 2 * D_FF), f32),
        "ffn_w_out": D_FF ** -0.5 * n(ks[10], (DEPTH, 2, D_FF, D_MODEL), f32),
        "w_in": D_MODEL ** -0.5 * n(ks[11], (DEPTH, D_MODEL, IN_COLS), f32),
        "w_four": FOUR_W ** -0.5 * n(ks[12], (DEPTH, FOUR_W, D_MODEL), f32),
        "hgrn_gain": 1.0 + 0.05 * n(ks[13], (DEPTH, HGRN_V_W), f32),
        "w_hgrn": HGRN_V_W ** -0.5 * n(ks[14], (DEPTH, HGRN_V_W, D_MODEL), f32),
        "w_out": D_MODEL ** -0.5 * n(ks[15], (DEPTH, D_MODEL, D_MODEL), f32),
        "lb_logits": 0.5 * n(ks[16], (DEPTH + 1, 2, HGRN_K_W), f32),
    }


def reference(x_prompt, x_sample, state_hgrn, c, c_ctx, w_mod, b_mod, norm_pre, norm_post,
              ffn_w_in, ffn_w_out, w_in, w_four, hgrn_gain, w_hgrn, w_out, lb_logits):
    lb_all = jnp.cumsum(jax.nn.softmax(lb_logits.astype(jnp.float32), axis=0), axis=0)
    rows = x_sample.shape[1] // GRID_W

    y_prompt = x_prompt
    zero_state = jnp.zeros((x_prompt.shape[0], N_HEADS, HEAD_K, HEAD_V), jnp.float32)
    layer_states = []
    for l in range(DEPTH):
        y_prompt, s_f, s_b = trunk_layer(
            y_prompt, c_ctx[None, :], w_mod[l], b_mod[l], norm_pre[l], norm_post[l],
            ffn_w_in[l], ffn_w_out[l], w_in[l], w_four[l], hgrn_gain[l], w_hgrn[l], w_out[l],
            lb_all[l], zero_state, zero_state, None)
        layer_states.append(jnp.stack([s_f, s_b], axis=1))
    new_state_hgrn = jnp.stack(layer_states, axis=1).astype(x_prompt.dtype)

    y_sample = x_sample
    for l in range(DEPTH):
        y_sample, _, _ = trunk_layer(
            y_sample, c, w_mod[l], b_mod[l], norm_pre[l], norm_post[l],
            ffn_w_in[l], ffn_w_out[l], w_in[l], w_four[l], hgrn_gain[l], w_hgrn[l], w_out[l],
            lb_all[l], state_hgrn[:, l, 0].astype(jnp.float32), state_hgrn[:, l, 1].astype(jnp.float32), rows)

    return (y_prompt, y_sample, new_state_hgrn)
```

```python
import functools

import jax
import jax.numpy as jnp
import numpy as np
from jax import lax
from jax.experimental import pallas as pl
from jax.experimental.pallas import tpu as pltpu

D_MODEL = 1024
D_FF = 2816
N_MOD = 9
EPS = 1e-6
GRID_W = 64
N_GROUPS = 4
GROUP = 128
FOUR_W = N_GROUPS * GROUP
N_HEADS = 4
HEAD = 128
HW = N_HEADS * HEAD
IN_COLS = FOUR_W + 3 * HW + 2 * HW + 2 * D_MODEL

F32 = jnp.float32
BF16 = jnp.bfloat16

VMEM_LIMIT_BYTES = 56 * 1024 * 1024
FF_CHUNK = 256
TOKEN_TILE = 512
HGRN_CHUNK = 128
MOD_ROWS = 16
MOD_COL_TILE = 1536


def _params(*sem):
    return pltpu.CompilerParams(dimension_semantics=sem, vmem_limit_bytes=VMEM_LIMIT_BYTES)


def _const_spec(shape):
    nd = len(shape)
    return pl.BlockSpec(shape, lambda *_: (0,) * nd, pipeline_mode=pl.Buffered(1))


def _sigmoid(x):
    return 1.0 / (1.0 + jnp.exp(-x))


def _silu(x):
    return x * _sigmoid(x)


def _rms(x, gain):
    return x * lax.rsqrt(jnp.mean(x * x, axis=-1, keepdims=True) + EPS) * gain


def _dot(a, b):
    return jnp.dot(a, b, preferred_element_type=F32)


def _dot_nt(a, b):
    return lax.dot_general(a, b, (((1,), (1,)), ((), ())), preferred_element_type=F32)


def _mod_kernel(cond_ref, w_ref, b_ref, out_ref):
    a = _silu(cond_ref[...]).astype(BF16)
    out_ref[...] = _dot(a, w_ref[...].astype(BF16)) + b_ref[...]


def _modulation(cond, w_mod, b_mod):
    n = w_mod.shape[1]
    return pl.pallas_call(
        _mod_kernel,
        out_shape=jax.ShapeDtypeStruct((MOD_ROWS, n), F32),
        grid=(n // MOD_COL_TILE,),
        in_specs=[
            pl.BlockSpec((MOD_ROWS, D_MODEL), lambda j: (0, 0)),
            pl.BlockSpec((D_MODEL, MOD_COL_TILE), lambda j: (0, j)),
            pl.BlockSpec((1, MOD_COL_TILE), lambda j: (0, j)),
        ],
        out_specs=pl.BlockSpec((MOD_ROWS, MOD_COL_TILE), lambda j: (0, j)),
        compiler_params=_params("parallel"),
        name="modulation",
    )(cond, w_mod, b_mod.reshape(1, n))


def _ffn_kernel(x_ref, mod_ref, npre_ref, npost_ref, win_ref, wout_ref, out_ref, acc_ref, *, mod_base):
    x = x_ref[...]
    sh = mod_ref[0, mod_base:mod_base + 1, :]
    sc = mod_ref[0, mod_base + 1:mod_base + 2, :]
    gm = mod_ref[0, mod_base + 2:mod_base + 3, :]
    h = (_rms(x, npre_ref[...]) * (1.0 + sc) + sh).astype(BF16)
    for j in range(D_FF // FF_CHUNK):
        lo = j * FF_CHUNK
        g = _dot(h, win_ref[:, lo:lo + FF_CHUNK])
        u = _dot(h, win_ref[:, D_FF + lo:D_FF + lo + FF_CHUNK])
        part = _dot((_silu(g) * u).astype(BF16), wout_ref[lo:lo + FF_CHUNK, :])
        if j == 0:
            acc_ref[...] = part
        else:
            acc_ref[...] += part
    out_ref[...] = x + 0.5 * gm * _rms(acc_ref[...], npost_ref[...])


def _ffn(x, mod, npre, npost, win, wout, *, mod_base, tiles_per_mod):
    t = x.shape[0]
    tm = TOKEN_TILE
    return pl.pallas_call(
        functools.partial(_ffn_kernel, mod_base=mod_base),
        out_shape=jax.ShapeDtypeStruct((t, D_MODEL), F32),
        grid=(t // tm,),
        in_specs=[
            pl.BlockSpec((tm, D_MODEL), lambda i: (i, 0)),
            pl.BlockSpec((1, N_MOD, D_MODEL), lambda i: (i // tiles_per_mod, 0, 0)),
            _const_spec((1, D_MODEL)),
            _const_spec((1, D_MODEL)),
            _const_spec((D_MODEL, 2 * D_FF)),
            _const_spec((D_FF, D_MODEL)),
        ],
        out_specs=pl.BlockSpec((tm, D_MODEL), lambda i: (i, 0)),
        scratch_shapes=[pltpu.VMEM((tm, D_MODEL), F32)],
        compiler_params=_params("parallel"),
        name="ffn_half_step",
    )(x, mod, npre, npost, win, wout)


def _proj_kernel(x_ref, mod_ref, npre_ref, lbl_ref, win_ref,
                 u_ref, q_ref, ff_ref, fb_ref, v_ref, g_ref, ga_ref, gb_ref):
    x = x_ref[...]
    sh = mod_ref[0, 3:4, :]
    sc = mod_ref[0, 4:5, :]
    h = (_rms(x, npre_ref[...]) * (1.0 + sc) + sh).astype(BF16)

    def cols(lo, n):
        return _dot(h, win_ref[:, lo:lo + n])

    l0 = lbl_ref[0]
    l1 = lbl_ref[1]
    mx = jnp.maximum(l0, l1)
    e0 = jnp.exp(l0 - mx)
    e1 = jnp.exp(l1 - mx)
    lb = e0 / (e0 + e1)

    u_ref[...] = cols(0, FOUR_W).astype(BF16)
    q_ref[...] = _silu(cols(FOUR_W, HW)).astype(BF16)
    ff_ref[...] = lb[0:1, :] + (1.0 - lb[0:1, :]) * _sigmoid(cols(FOUR_W + HW, HW))
    fb_ref[...] = lb[1:2, :] + (1.0 - lb[1:2, :]) * _sigmoid(cols(FOUR_W + 2 * HW, HW))
    v_ref[...] = cols(FOUR_W + 3 * HW, HW).astype(BF16)
    g_ref[...] = _silu(cols(FOUR_W + 4 * HW, HW)).astype(BF16)
    ga_ref[...] = _sigmoid(cols(FOUR_W + 5 * HW, D_MODEL)).astype(BF16)
    gb_ref[...] = _sigmoid(cols(FOUR_W + 5 * HW + D_MODEL, D_MODEL)).astype(BF16)


def _proj(x, mod, npre, lb_logits, win, *, tiles_per_mod):
    t = x.shape[0]
    tm = TOKEN_TILE

    def tok(n):
        return pl.BlockSpec((tm, n), lambda i: (i, 0))

    shapes = [
        jax.ShapeDtypeStruct((t, FOUR_W), BF16),
        jax.ShapeDtypeStruct((t, HW), BF16),
        jax.ShapeDtypeStruct((t, HW), F32),
        jax.ShapeDtypeStruct((t, HW), F32),
        jax.ShapeDtypeStruct((t, HW), BF16),
        jax.ShapeDtypeStruct((t, HW), BF16),
        jax.ShapeDtypeStruct((t, D_MODEL), BF16),
        jax.ShapeDtypeStruct((t, D_MODEL), BF16),
    ]
    return pl.pallas_call(
        _proj_kernel,
        out_shape=shapes,
        grid=(t // tm,),
        in_specs=[
            tok(D_MODEL),
            pl.BlockSpec((1, N_MOD, D_MODEL), lambda i: (i // tiles_per_mod, 0, 0)),
            _const_spec((1, D_MODEL)),
            _const_spec((2, 2, HW)),
            _const_spec((D_MODEL, IN_COLS)),
        ],
        out_specs=[tok(FOUR_W), tok(HW), tok(HW), tok(HW), tok(HW), tok(HW), tok(D_MODEL), tok(D_MODEL)],
        compiler_params=_params("parallel"),
        name="mixer_in_proj",
    )(x, mod, npre, lb_logits, win)


def _cos_sin(n):
    k = np.arange(n)
    ang = 2.0 * np.pi * ((k[:, None] * k[None, :]) % n) / n
    return np.cos(ang), np.sin(ang)


def _bf16_const(a):
    return jnp.asarray(a, F32).astype(BF16)


def _channel_dft(u):
    def go(cs_ref):
        a_parts, b_parts = [], []
        for g in range(N_GROUPS):
            ab = _dot(u[:, g * GROUP:(g + 1) * GROUP], cs_ref[...])
            a_parts.append(ab[:, :GROUP])
            b_parts.append(ab[:, GROUP:])
        return (jnp.concatenate(a_parts, axis=1).astype(BF16),
                jnp.concatenate(b_parts, axis=1).astype(BF16))
    return go


def _fourier_seq_kernel(u_ref, cs_ref, pos_ref, y_ref, *, seq, n_seq):
    for s in range(n_seq):
        a, b = _channel_dft(u_ref[s * seq:(s + 1) * seq, :])(cs_ref)
        ab = jnp.concatenate([a, b], axis=0)
        y_ref[s * seq:(s + 1) * seq, :] = _dot(pos_ref[...], ab).astype(BF16)


def _fourier_seq(u, seq):
    t = u.shape[0]
    n_seq = 4
    cc, sc = _cos_sin(GROUP)
    cl, sl = _cos_sin(seq)
    cs = _bf16_const(np.concatenate([cc, sc], axis=1) / np.sqrt(GROUP))
    pos = _bf16_const(np.concatenate([cl, -sl], axis=1) / np.sqrt(seq))
    rows = n_seq * seq
    return pl.pallas_call(
        functools.partial(_fourier_seq_kernel, seq=seq, n_seq=n_seq),
        out_shape=jax.ShapeDtypeStruct((t, FOUR_W), BF16),
        grid=(t // rows,),
        in_specs=[
            pl.BlockSpec((rows, FOUR_W), lambda i: (i, 0)),
            _const_spec((GROUP, 2 * GROUP)),
            _const_spec((seq, 2 * seq)),
        ],
        out_specs=pl.BlockSpec((rows, FOUR_W), lambda i: (i, 0)),
        compiler_params=_params("parallel"),
        name="fourier_seq",
    )(u, cs, pos)


W_BLOCK = 16


def _fourier_grid_kernel(u_ref, cs_ref, m1_ref, k2_ref, y_ref, p_ref, q_ref, *, n_rows):
    def row_body(r, carry):
        start = pl.multiple_of(r * GRID_W, GRID_W)
        a, b = _channel_dft(u_ref[pl.ds(start, GRID_W), :])(cs_ref)
        ab = jnp.concatenate([a, b], axis=0)
        pq = _dot(m1_ref[...], ab)
        p_ref[r] = pq[:GRID_W]
        q_ref[r] = pq[GRID_W:]
        return carry

    lax.fori_loop(0, n_rows, row_body, 0)

    for j in range(GRID_W // W_BLOCK):
        ws = slice(j * W_BLOCK, (j + 1) * W_BLOCK)
        pb = p_ref[:, ws, :].reshape(n_rows * W_BLOCK, FOUR_W).astype(BF16)
        qb = q_ref[:, ws, :].reshape(n_rows * W_BLOCK, FOUR_W).astype(BF16)
        yb = _dot(k2_ref[...], jnp.concatenate([pb, qb], axis=0))
        y_ref[0, :, ws, :] = yb.reshape(n_rows, W_BLOCK, FOUR_W).astype(BF16)


def _fourier_grid(u, n_batch, n_rows):
    cc, sc = _cos_sin(GROUP)
    cw, sw = _cos_sin(GRID_W)
    cr, sr = _cos_sin(n_rows)
    cs = _bf16_const(np.concatenate([cc, sc], axis=1) / np.sqrt(GROUP))
    m1 = _bf16_const(np.block([[cw, -sw], [sw, cw]]) / np.sqrt(GRID_W))
    eye = np.eye(W_BLOCK)
    k2 = _bf16_const(np.concatenate([np.kron(cr, eye), -np.kron(sr, eye)], axis=1) / np.sqrt(n_rows))
    seq = n_rows * GRID_W
    nk = n_rows * W_BLOCK
    y = pl.pallas_call(
        functools.partial(_fourier_grid_kernel, n_rows=n_rows),
        out_shape=jax.ShapeDtypeStruct((n_batch, n_rows, GRID_W, FOUR_W), BF16),
        grid=(n_batch,),
        in_specs=[
            pl.BlockSpec((seq, FOUR_W), lambda b: (b, 0)),
            _const_spec((GROUP, 2 * GROUP)),
            _const_spec((2 * GRID_W, 2 * GRID_W)),
            _const_spec((nk, 2 * nk)),
        ],
        out_specs=pl.BlockSpec((1, n_rows, GRID_W, FOUR_W), lambda b: (b, 0, 0, 0)),
        scratch_shapes=[pltpu.VMEM((n_rows, GRID_W, FOUR_W), F32),
                        pltpu.VMEM((n_rows, GRID_W, FOUR_W), F32)],
        compiler_params=_params("parallel"),
        name="fourier_grid",
    )(u, cs, m1, k2)
    return y.reshape(n_batch * seq, FOUR_W)


def _swap_blocks(x, h, bit):
    c = x.shape[0]
    if h >= 8:
        parts = [x[(b ^ 1) * h:((b ^ 1) + 1) * h] for b in range(c // h)]
        return jnp.concatenate(parts, axis=0)
    return jnp.where(bit, pltpu.roll(x, h, 0), pltpu.roll(x, c - h, 0))


def _hgrn_kernel(*refs, chunk, reverse, zero_init, final):
    it = iter(refs)
    q_ref, f_ref, v_ref = next(it), next(it), next(it)
    s0_ref = None if zero_init else next(it)
    if final:
        oprev_ref, gs_ref, gain_ref = next(it), next(it), next(it)
    o_ref, sfin_ref, st_ref = next(it), next(it), next(it)

    i = pl.program_id(1)
    c = chunk

    @pl.when(i == 0)
    def _():
        for hd in range(N_HEADS):
            if zero_init:
                st_ref[hd] = jnp.zeros((HEAD, HEAD), F32)
            else:
                st_ref[hd] = s0_ref[0, 0, 0, hd].T

    row = lax.broadcasted_iota(jnp.int32, (c, HEAD), 0)
    tt = lax.broadcasted_iota(jnp.int32, (c, c), 0)
    ss = lax.broadcasted_iota(jnp.int32, (c, c), 1)
    xor = tt ^ ss
    ordered = (tt < ss) if reverse else (tt > ss)
    levels = [1 << b for b in range(c.bit_length() - 1)]
    level_mask = {h: ordered & (xor >= h) & (xor < 2 * h) for h in levels}
    diag_mask = tt == ss
    ones = jnp.ones((c, HEAD), F32)

    for hd in range(N_HEADS):
        sl = slice(hd * HEAD, (hd + 1) * HEAD)
        f = f_ref[:, sl]
        q = q_ref[:, sl].astype(F32)
        v = v_ref[:, sl]
        k = 1.0 - f
        p, r, tot = f, ones, f
        a = jnp.zeros((c, c), F32)
        for h in levels:
            qk = _dot_nt((q * p).astype(BF16), (k * r).astype(BF16))
            a = jnp.where(level_mask[h], qk, a)
            bit = (row & h) != 0
            w = _swap_blocks(tot, h, bit)
            grow_p = bit != reverse
            p = p * jnp.where(grow_p, w, 1.0)
            r = r * jnp.where(grow_p, 1.0, w)
            tot = tot * w
        a = jnp.where(diag_mask, jnp.sum(q * k, axis=-1, keepdims=True), a)
        st = st_ref[hd]
        o = _dot(a.astype(BF16), v) + _dot_nt((q * p).astype(BF16), st.astype(BF16))
        vt = v.astype(F32).T.astype(BF16)
        st_new = st * tot[0:1, :] + _dot(vt, (k * r).astype(BF16))
        st_ref[hd] = st_new
        if final:
            o = o + oprev_ref[:, sl]
            o = o * lax.rsqrt(jnp.mean(o * o, axis=-1, keepdims=True) + EPS)
            o = o * gain_ref[:, sl] * gs_ref[:, sl].astype(F32)
            o_ref[:, sl] = o.astype(BF16)
        else:
            o_ref[:, sl] = o

    @pl.when(i == pl.num_programs(1) - 1)
    def _():
        for hd in range(N_HEADS):
            sfin_ref[0, hd] = st_ref[hd].T


def _hgrn(q, f, v, *, n_batch, seq, reverse, s0=None, s0_dir=0, final_inputs=None):
    c = HGRN_CHUNK
    n_chunks = seq // c
    final = final_inputs is not None

    def tok_map(b, i):
        ci = (n_chunks - 1 - i) if reverse else i
        return (b * n_chunks + ci, 0)

    tok = pl.BlockSpec((c, HW), tok_map)
    in_specs = [tok, tok, tok]
    args = [q, f, v]
    if s0 is not None:
        in_specs.append(pl.BlockSpec((1, 1, 1, N_HEADS, HEAD, HEAD), lambda b, i: (b, 0, s0_dir, 0, 0, 0)))
        args.append(s0)
    if final:
        oprev, gs, gain = final_inputs
        in_specs += [tok, tok, pl.BlockSpec((1, HW), lambda b, i: (0, 0))]
        args += [oprev, gs, gain]
    t = q.shape[0]
    out_shape = [jax.ShapeDtypeStruct((t, HW), BF16 if final else F32),
                 jax.ShapeDtypeStruct((n_batch, N_HEADS, HEAD, HEAD), F32)]
    out_specs = [tok, pl.BlockSpec((1, N_HEADS, HEAD, HEAD), lambda b, i: (b, 0, 0, 0))]
    return pl.pallas_call(
        functools.partial(_hgrn_kernel, chunk=c, reverse=reverse, zero_init=s0 is None, final=final),
        out_shape=out_shape,
        grid=(n_batch, n_chunks),
        in_specs=in_specs,
        out_specs=out_specs,
        scratch_shapes=[pltpu.VMEM((N_HEADS, HEAD, HEAD), F32)],
        compiler_params=_params("arbitrary", "arbitrary"),
        name="hgrn2_bwd" if reverse else "hgrn2_fwd",
    )(*args)


def _merge_kernel(x_ref, y_ref, o_ref, ga_ref, gb_ref, mod_ref, npost_ref, wf_ref, wh_ref, wo_ref, out_ref):
    ba = _dot(y_ref[...], wf_ref[...])
    bb = _dot(o_ref[...], wh_ref[...])
    merged = ga_ref[...].astype(F32) * ba + gb_ref[...].astype(F32) * bb
    m = _dot(merged.astype(BF16), wo_ref[...])
    out_ref[...] = x_ref[...] + mod_ref[0, 5:6, :] * _rms(m, npost_ref[...])


def _merge(x, y, o, ga, gb, mod, npost, wf, wh, wo, *, tiles_per_mod):
    t = x.shape[0]
    tm = TOKEN_TILE

    def tok(n):
        return pl.BlockSpec((tm, n), lambda i: (i, 0))

    return pl.pallas_call(
        _merge_kernel,
        out_shape=jax.ShapeDtypeStruct((t, D_MODEL), F32),
        grid=(t // tm,),
        in_specs=[
            tok(D_MODEL), tok(FOUR_W), tok(HW), tok(D_MODEL), tok(D_MODEL),
            pl.BlockSpec((1, N_MOD, D_MODEL), lambda i: (i // tiles_per_mod, 0, 0)),
            _const_spec((1, D_MODEL)),
            _const_spec((FOUR_W, D_MODEL)),
            _const_spec((HW, D_MODEL)),
            _const_spec((D_MODEL, D_MODEL)),
        ],
        out_specs=tok(D_MODEL),
        compiler_params=_params("parallel"),
        name="mixer_merge",
    )(x, y, o, ga, gb, mod, npost, wf, wh, wo)


def _trunk_layer(x, mod, w, *, n_batch, seq, grid_rows, s0):
    n_tiles = x.shape[0] // TOKEN_TILE
    tiles_per_mod = n_tiles // mod.shape[0]
    x1 = _ffn(x, mod, w["npre"][0], w["npost"][0], w["ffn_in"][0], w["ffn_out"][0],
              mod_base=0, tiles_per_mod=tiles_per_mod)
    u, q, ff, fb, v, gs, ga, gb = _proj(x1, mod, w["npre"][1], w["lb_logits"], w["w_in"],
                                        tiles_per_mod=tiles_per_mod)
    if grid_rows is None:
        y = _fourier_seq(u, seq)
    else:
        y = _fourier_grid(u, n_batch, grid_rows)
    o_f, s_f = _hgrn(q, ff, v, n_batch=n_batch, seq=seq, reverse=False, s0=s0, s0_dir=0)
    o, s_b = _hgrn(q, fb, v, n_batch=n_batch, seq=seq, reverse=True, s0=s0, s0_dir=1,
                   final_inputs=(o_f, gs, w["hgrn_gain"]))
    x2 = _merge(x1, y, o, ga, gb, mod, w["npost"][1], w["w_four"], w["w_hgrn"], w["w_out"],
                tiles_per_mod=tiles_per_mod)
    out = _ffn(x2, mod, w["npre"][2], w["npost"][2], w["ffn_in"][1], w["ffn_out"][1],
               mod_base=6, tiles_per_mod=tiles_per_mod)
    return out, s_f, s_b


def kernel(x_prompt, x_sample, state_hgrn, c, c_ctx, w_mod, b_mod, norm_pre, norm_post, ffn_w_in, ffn_w_out,
           w_in, w_four, hgrn_gain, w_hgrn, w_out, lb_logits):
    batch, seq, d = x_prompt.shape
    dec_batch, dec_seq, _ = x_sample.shape
    depth = w_mod.shape[0]
    assert depth == 1 and d == D_MODEL
    rows = dec_seq // GRID_W
    layer = 0

    cond = jnp.concatenate([c_ctx[None, :], c, jnp.zeros((MOD_ROWS - 1 - dec_batch, d), F32)], axis=0)
    mod = _modulation(cond, w_mod[layer], b_mod[layer]).reshape(MOD_ROWS, N_MOD, d)
    mod_ctx = mod[0:1]
    mod_dec = mod[1:1 + dec_batch]

    w = dict(
        npre=norm_pre[layer].reshape(3, 1, d),
        npost=norm_post[layer].reshape(3, 1, d),
        ffn_in=ffn_w_in[layer].astype(BF16),
        ffn_out=ffn_w_out[layer].astype(BF16),
        w_in=w_in[layer].astype(BF16),
        w_four=w_four[layer].astype(BF16),
        w_hgrn=w_hgrn[layer].astype(BF16),
        w_out=w_out[layer].astype(BF16),
        hgrn_gain=hgrn_gain[layer].reshape(1, HW),
        lb_logits=lb_logits[layer:layer + 2],
    )

    y_prompt, s_f, s_b = _trunk_layer(x_prompt.reshape(batch * seq, d), mod_ctx, w,
                                      n_batch=batch, seq=seq, grid_rows=None, s0=None)
    new_state = jnp.stack([s_f, s_b], axis=1)[:, None].astype(x_prompt.dtype)

    y_sample, _, _ = _trunk_layer(x_sample.reshape(dec_batch * dec_seq, d), mod_dec, w,
                                  n_batch=dec_batch, seq=dec_seq, grid_rows=rows, s0=state_hgrn)
    return (y_prompt.reshape(batch, seq, d), y_sample.reshape(dec_batch, dec_seq, d), new_state)
```

```python
import functools

import jax
import jax.numpy as jnp
import numpy as np
from jax import lax
from jax.experimental import pallas as pl
from jax.experimental.pallas import tpu as pltpu

D_MODEL = 1024
D_FF = 2816
N_MOD = 9
EPS = 1e-6
GRID_W = 64
N_GROUPS = 4
GROUP = 128
FOUR_W = N_GROUPS * GROUP
N_HEADS = 4
HEAD = 128
HW = N_HEADS * HEAD
IN_COLS = FOUR_W + 3 * HW + 2 * HW + 2 * D_MODEL

F32 = jnp.float32
BF16 = jnp.bfloat16

VMEM_LIMIT_BYTES = 56 * 1024 * 1024
FF_CHUNK = 256
TOKEN_TILE = 512
HGRN_CHUNK = 128
HGRN_SEQS = 2
MOD_ROWS = 16
MOD_COL_TILE = 1536


def _params(*sem):
    return pltpu.CompilerParams(dimension_semantics=sem, vmem_limit_bytes=VMEM_LIMIT_BYTES)


def _const_spec(shape):
    nd = len(shape)
    return pl.BlockSpec(shape, lambda *_: (0,) * nd, pipeline_mode=pl.Buffered(1))


def _sigmoid(x):
    return 1.0 / (1.0 + jnp.exp(-x))


def _silu(x):
    return x * _sigmoid(x)


def _rms(x, gain):
    return x * lax.rsqrt(jnp.mean(x * x, axis=-1, keepdims=True) + EPS) * gain


def _dot(a, b):
    return jnp.dot(a, b, preferred_element_type=F32)


def _dot_nt(a, b):
    return lax.dot_general(a, b, (((1,), (1,)), ((), ())), preferred_element_type=F32)


def _mod_kernel(cond_ref, w_ref, b_ref, out_ref):
    a = _silu(cond_ref[...]).astype(BF16)
    out_ref[...] = _dot(a, w_ref[...].astype(BF16)) + b_ref[...]


def _modulation(cond, w_mod, b_mod):
    n = w_mod.shape[1]
    return pl.pallas_call(
        _mod_kernel,
        out_shape=jax.ShapeDtypeStruct((MOD_ROWS, n), F32),
        grid=(n // MOD_COL_TILE,),
        in_specs=[
            pl.BlockSpec((MOD_ROWS, D_MODEL), lambda j: (0, 0)),
            pl.BlockSpec((D_MODEL, MOD_COL_TILE), lambda j: (0, j)),
            pl.BlockSpec((1, MOD_COL_TILE), lambda j: (0, j)),
        ],
        out_specs=pl.BlockSpec((MOD_ROWS, MOD_COL_TILE), lambda j: (0, j)),
        compiler_params=_params("parallel"),
        name="modulation",
    )(cond, w_mod, b_mod.reshape(1, n))


def _ffn_kernel(x_ref, mod_ref, npre_ref, npost_ref, win_ref, wout_ref, out_ref, acc_ref, *, mod_base):
    x = x_ref[...]
    sh = mod_ref[0, mod_base:mod_base + 1, :]
    sc = mod_ref[0, mod_base + 1:mod_base + 2, :]
    gm = mod_ref[0, mod_base + 2:mod_base + 3, :]
    h = (_rms(x, npre_ref[...]) * (1.0 + sc) + sh).astype(BF16)
    for j in range(D_FF // FF_CHUNK):
        lo = j * FF_CHUNK
        g = _dot(h, win_ref[:, lo:lo + FF_CHUNK])
        u = _dot(h, win_ref[:, D_FF + lo:D_FF + lo + FF_CHUNK])
        part = _dot((_silu(g) * u).astype(BF16), wout_ref[lo:lo + FF_CHUNK, :])
        if j == 0:
            acc_ref[...] = part
        else:
            acc_ref[...] += part
    out_ref[...] = x + 0.5 * gm * _rms(acc_ref[...], npost_ref[...])


def _ffn(x, mod, npre, npost, win, wout, *, mod_base, tiles_per_mod):
    t = x.shape[0]
    tm = TOKEN_TILE
    return pl.pallas_call(
        functools.partial(_ffn_kernel, mod_base=mod_base),
        out_shape=jax.ShapeDtypeStruct((t, D_MODEL), F32),
        grid=(t // tm,),
        in_specs=[
            pl.BlockSpec((tm, D_MODEL), lambda i: (i, 0)),
            pl.BlockSpec((1, N_MOD, D_MODEL), lambda i: (i // tiles_per_mod, 0, 0)),
            _const_spec((1, D_MODEL)),
            _const_spec((1, D_MODEL)),
            _const_spec((D_MODEL, 2 * D_FF)),
            _const_spec((D_FF, D_MODEL)),
        ],
        out_specs=pl.BlockSpec((tm, D_MODEL), lambda i: (i, 0)),
        scratch_shapes=[pltpu.VMEM((tm, D_MODEL), F32)],
        compiler_params=_params("parallel"),
        name="ffn_half_step",
    )(x, mod, npre, npost, win, wout)


SUBLANES = 8
CHUNK_VREGS = HGRN_CHUNK // SUBLANES


def _chunk_token_of_row(row):
    return (row & (SUBLANES - 1)) * CHUNK_VREGS + (row >> 3)


def _chunk_perm_matrix(transpose):
    c = HGRN_CHUNK
    a = lax.broadcasted_iota(jnp.int32, (c, c), 0)
    b = lax.broadcasted_iota(jnp.int32, (c, c), 1)
    hit = (a == _chunk_token_of_row(b)) if transpose else (b == _chunk_token_of_row(a))
    return jnp.where(hit, 1.0, 0.0).astype(BF16)


def _proj_kernel(x_ref, mod_ref, npre_ref, lbl_ref, win_ref,
                 u_ref, q_ref, ff_ref, fb_ref, v_ref, g_ref, ga_ref, gb_ref):
    x = x_ref[...]
    sh = mod_ref[0, 3:4, :]
    sc = mod_ref[0, 4:5, :]
    h = (_rms(x, npre_ref[...]) * (1.0 + sc) + sh).astype(BF16)
    perm = _chunk_perm_matrix(transpose=False)
    c = HGRN_CHUNK
    hp = jnp.concatenate([_dot(perm, h[j * c:(j + 1) * c]) for j in range(h.shape[0] // c)],
                         axis=0).astype(BF16)

    def cols(lhs, lo, n):
        return _dot(lhs, win_ref[:, lo:lo + n])

    l0 = lbl_ref[0]
    l1 = lbl_ref[1]
    mx = jnp.maximum(l0, l1)
    e0 = jnp.exp(l0 - mx)
    e1 = jnp.exp(l1 - mx)
    lb = e0 / (e0 + e1)

    u_ref[...] = cols(h, 0, FOUR_W).astype(BF16)
    q_ref[...] = _silu(cols(hp, FOUR_W, HW)).astype(BF16)
    ff_ref[...] = lb[0:1, :] + (1.0 - lb[0:1, :]) * _sigmoid(cols(hp, FOUR_W + HW, HW))
    fb_ref[...] = lb[1:2, :] + (1.0 - lb[1:2, :]) * _sigmoid(cols(hp, FOUR_W + 2 * HW, HW))
    v_ref[...] = cols(hp, FOUR_W + 3 * HW, HW).astype(BF16)
    g_ref[...] = _silu(cols(hp, FOUR_W + 4 * HW, HW)).astype(BF16)
    ga_ref[...] = _sigmoid(cols(h, FOUR_W + 5 * HW, D_MODEL)).astype(BF16)
    gb_ref[...] = _sigmoid(cols(h, FOUR_W + 5 * HW + D_MODEL, D_MODEL)).astype(BF16)


def _proj(x, mod, npre, lb_logits, win, *, tiles_per_mod):
    t = x.shape[0]
    tm = TOKEN_TILE

    def tok(n):
        return pl.BlockSpec((tm, n), lambda i: (i, 0))

    shapes = [
        jax.ShapeDtypeStruct((t, FOUR_W), BF16),
        jax.ShapeDtypeStruct((t, HW), BF16),
        jax.ShapeDtypeStruct((t, HW), F32),
        jax.ShapeDtypeStruct((t, HW), F32),
        jax.ShapeDtypeStruct((t, HW), BF16),
        jax.ShapeDtypeStruct((t, HW), BF16),
        jax.ShapeDtypeStruct((t, D_MODEL), BF16),
        jax.ShapeDtypeStruct((t, D_MODEL), BF16),
    ]
    return pl.pallas_call(
        _proj_kernel,
        out_shape=shapes,
        grid=(t // tm,),
        in_specs=[
            tok(D_MODEL),
            pl.BlockSpec((1, N_MOD, D_MODEL), lambda i: (i // tiles_per_mod, 0, 0)),
            _const_spec((1, D_MODEL)),
            _const_spec((2, 2, HW)),
            _const_spec((D_MODEL, IN_COLS)),
        ],
        out_specs=[tok(FOUR_W), tok(HW), tok(HW), tok(HW), tok(HW), tok(HW), tok(D_MODEL), tok(D_MODEL)],
        compiler_params=_params("parallel"),
        name="mixer_in_proj",
    )(x, mod, npre, lb_logits, win)


def _cos_sin(n):
    k = np.arange(n)
    ang = 2.0 * np.pi * ((k[:, None] * k[None, :]) % n) / n
    return np.cos(ang), np.sin(ang)


def _bf16_const(a):
    return jnp.asarray(a, F32).astype(BF16)


def _channel_dft(u):
    def go(cs_ref):
        a_parts, b_parts = [], []
        for g in range(N_GROUPS):
            ab = _dot(u[:, g * GROUP:(g + 1) * GROUP], cs_ref[...])
            a_parts.append(ab[:, :GROUP])
            b_parts.append(ab[:, GROUP:])
        return (jnp.concatenate(a_parts, axis=1).astype(BF16),
                jnp.concatenate(b_parts, axis=1).astype(BF16))
    return go


def _fourier_seq_kernel(u_ref, cs_ref, pos_ref, y_ref, *, seq, n_seq):
    for s in range(n_seq):
        a, b = _channel_dft(u_ref[s * seq:(s + 1) * seq, :])(cs_ref)
        ab = jnp.concatenate([a, b], axis=0)
        y_ref[s * seq:(s + 1) * seq, :] = _dot(pos_ref[...], ab).astype(BF16)


def _fourier_seq(u, seq):
    t = u.shape[0]
    n_seq = 4
    cc, sc = _cos_sin(GROUP)
    cl, sl = _cos_sin(seq)
    cs = _bf16_const(np.concatenate([cc, sc], axis=1) / np.sqrt(GROUP))
    pos = _bf16_const(np.concatenate([cl, -sl], axis=1) / np.sqrt(seq))
    rows = n_seq * seq
    return pl.pallas_call(
        functools.partial(_fourier_seq_kernel, seq=seq, n_seq=n_seq),
        out_shape=jax.ShapeDtypeStruct((t, FOUR_W), BF16),
        grid=(t // rows,),
        in_specs=[
            pl.BlockSpec((rows, FOUR_W), lambda i: (i, 0)),
            _const_spec((GROUP, 2 * GROUP)),
            _const_spec((seq, 2 * seq)),
        ],
        out_specs=pl.BlockSpec((rows, FOUR_W), lambda i: (i, 0)),
        compiler_params=_params("parallel"),
        name="fourier_seq",
    )(u, cs, pos)


W_BLOCK = 16


def _fourier_grid_kernel(u_ref, cs_ref, m1_ref, k2_ref, y_ref, p_ref, q_ref, *, n_rows):
    def row_body(r, carry):
        start = pl.multiple_of(r * GRID_W, GRID_W)
        a, b = _channel_dft(u_ref[pl.ds(start, GRID_W), :])(cs_ref)
        ab = jnp.concatenate([a, b], axis=0)
        pq = _dot(m1_ref[...], ab)
        p_ref[r] = pq[:GRID_W]
        q_ref[r] = pq[GRID_W:]
        return carry

    lax.fori_loop(0, n_rows, row_body, 0)

    for j in range(GRID_W // W_BLOCK):
        ws = slice(j * W_BLOCK, (j + 1) * W_BLOCK)
        pb = p_ref[:, ws, :].reshape(n_rows * W_BLOCK, FOUR_W).astype(BF16)
        qb = q_ref[:, ws, :].reshape(n_rows * W_BLOCK, FOUR_W).astype(BF16)
        yb = _dot(k2_ref[...], jnp.concatenate([pb, qb], axis=0))
        y_ref[0, :, ws, :] = yb.reshape(n_rows, W_BLOCK, FOUR_W).astype(BF16)


def _fourier_grid(u, n_batch, n_rows):
    cc, sc = _cos_sin(GROUP)
    cw, sw = _cos_sin(GRID_W)
    cr, sr = _cos_sin(n_rows)
    cs = _bf16_const(np.concatenate([cc, sc], axis=1) / np.sqrt(GROUP))
    m1 = _bf16_const(np.block([[cw, -sw], [sw, cw]]) / np.sqrt(GRID_W))
    eye = np.eye(W_BLOCK)
    k2 = _bf16_const(np.concatenate([np.kron(cr, eye), -np.kron(sr, eye)], axis=1) / np.sqrt(n_rows))
    seq = n_rows * GRID_W
    nk = n_rows * W_BLOCK
    y = pl.pallas_call(
        functools.partial(_fourier_grid_kernel, n_rows=n_rows),
        out_shape=jax.ShapeDtypeStruct((n_batch, n_rows, GRID_W, FOUR_W), BF16),
        grid=(n_batch,),
        in_specs=[
            pl.BlockSpec((seq, FOUR_W), lambda b: (b, 0)),
            _const_spec((GROUP, 2 * GROUP)),
            _const_spec((2 * GRID_W, 2 * GRID_W)),
            _const_spec((nk, 2 * nk)),
        ],
        out_specs=pl.BlockSpec((1, n_rows, GRID_W, FOUR_W), lambda b: (b, 0, 0, 0)),
        scratch_shapes=[pltpu.VMEM((n_rows, GRID_W, FOUR_W), F32),
                        pltpu.VMEM((n_rows, GRID_W, FOUR_W), F32)],
        compiler_params=_params("parallel"),
        name="fourier_grid",
    )(u, cs, m1, k2)
    return y.reshape(n_batch * seq, FOUR_W)


def _hgrn_kernel(*refs, reverse, zero_init, final, n_seq):
    it = iter(refs)
    q_ref, f_ref, v_ref = next(it), next(it), next(it)
    s0_ref = None if zero_init else next(it)
    if final:
        oprev_ref, gs_ref, gain_ref = next(it), next(it), next(it)
    o_ref, sfin_ref, st_ref = next(it), next(it), next(it)

    i = pl.program_id(1)
    c = HGRN_CHUNK
    nv = CHUNK_VREGS

    @pl.when(i == 0)
    def _():
        for sq in range(n_seq):
            for hd in range(N_HEADS):
                if zero_init:
                    st_ref[sq * N_HEADS + hd] = jnp.zeros((HEAD, HEAD), F32)
                else:
                    st_ref[sq * N_HEADS + hd] = s0_ref[sq, 0, 0, hd].T

    row_a = lax.broadcasted_iota(jnp.int32, (c, c), 0)
    row_b = lax.broadcasted_iota(jnp.int32, (c, c), 1)
    tt = _chunk_token_of_row(row_a)
    ss = _chunk_token_of_row(row_b)
    xor = tt ^ ss
    ordered = (tt < ss) if reverse else (tt > ss)
    vreg_levels = [1 << b for b in range(nv.bit_length() - 1)]
    sub_levels = [1 << b for b in range(SUBLANES.bit_length() - 1)]
    level_mask = {h: ordered & (xor >= h) & (xor < 2 * h) for h in vreg_levels + [nv * g for g in sub_levels]}
    diag_mask = row_a == row_b
    sub = lax.broadcasted_iota(jnp.int32, (SUBLANES, HEAD), 0)
    out_bit = 0 if reverse else 1
    if final:
        unperm = _chunk_perm_matrix(transpose=True)

    def vregs(x):
        return [x[SUBLANES * j:SUBLANES * (j + 1)] for j in range(nv)]

    def operand(xs):
        return jnp.concatenate(xs, axis=0).astype(BF16)

    def times(x, m):
        return m if x is None else x * m

    for sq, hd in [(a_, b_) for a_ in range(n_seq) for b_ in range(N_HEADS)]:
        sl = slice(hd * HEAD, (hd + 1) * HEAD)
        slot = sq * N_HEADS + hd
        f = f_ref[sq, :, sl]
        q = q_ref[sq, :, sl].astype(F32)
        vb = v_ref[sq, :, sl]
        k = 1.0 - f
        qv, kv = vregs(q), vregs(k)
        p = vregs(f)
        r = [None] * nv
        tot = vregs(f)
        a = jnp.zeros((c, c), F32)
        for h in vreg_levels:
            is_out = [((x // h) & 1) == out_bit for x in range(nv)]
            qop = operand([qv[x] * p[x] if is_out[x] else qv[x] for x in range(nv)])
            kop = operand([kv[x] if is_out[x] else times(r[x], kv[x]) for x in range(nv)])
            a = jnp.where(level_mask[h], _dot_nt(qop, kop), a)
            new_tot = []
            for b2 in range(len(tot) // 2):
                lo, hi = tot[2 * b2], tot[2 * b2 + 1]
                lo_rows = range(2 * b2 * h, (2 * b2 + 1) * h)
                hi_rows = range((2 * b2 + 1) * h, (2 * b2 + 2) * h)
                if reverse:
                    for x in lo_rows:
                        p[x] = p[x] * hi
                    for x in hi_rows:
                        r[x] = times(r[x], lo)
                else:
                    for x in hi_rows:
                        p[x] = p[x] * lo
                    for x in lo_rows:
                        r[x] = times(r[x], hi)
                new_tot.append(lo * hi)
            tot = new_tot
        t8 = tot[0]
        for g in sub_levels:
            qop = operand([qv[x] * p[x] for x in range(nv)])
            kop = operand([times(r[x], kv[x]) for x in range(nv)])
            a = jnp.where(level_mask[nv * g], _dot_nt(qop, kop), a)
            hi_half = (sub & g) != 0
            w = jnp.where(hi_half, pltpu.roll(t8, g, 0), pltpu.roll(t8, SUBLANES - g, 0))
            grow_p = hi_half != reverse
            mp = jnp.where(grow_p, w, 1.0)
            mr = jnp.where(grow_p, 1.0, w)
            p = [x * mp for x in p]
            r = [times(x, mr) for x in r]
            t8 = t8 * w
        a = jnp.where(diag_mask, jnp.sum(q * k, axis=-1, keepdims=True), a)
        st = st_ref[slot]
        qdec = operand([qv[x] * p[x] for x in range(nv)])
        kdec = operand([kv[x] * r[x] for x in range(nv)])
        o = _dot(a.astype(BF16), vb) + _dot_nt(qdec, st.astype(BF16))
        vt = vb.astype(F32).T.astype(BF16)
        st_ref[slot] = st * t8[0:1, :] + _dot(vt, kdec)
        if final:
            o = o + oprev_ref[sq, :, sl]
            o = o * lax.rsqrt(jnp.mean(o * o, axis=-1, keepdims=True) + EPS)
            o = o * gain_ref[:, sl] * gs_ref[sq, :, sl].astype(F32)
            o_ref[sq, :, sl] = _dot(unperm, o.astype(BF16)).astype(BF16)
        else:
            o_ref[sq, :, sl] = o

    @pl.when(i == pl.num_programs(1) - 1)
    def _():
        for sq in range(n_seq):
            for hd in range(N_HEADS):
                sfin_ref[sq, hd] = st_ref[sq * N_HEADS + hd].T


def _hgrn(q, f, v, *, n_batch, seq, reverse, s0=None, s0_dir=0, final_inputs=None):
    c = HGRN_CHUNK
    ns = HGRN_SEQS
    n_chunks = seq // c
    final = final_inputs is not None

    def tok_map(b, i):
        return (b, (n_chunks - 1 - i) if reverse else i, 0)

    tok = pl.BlockSpec((ns, c, HW), tok_map)
    in_specs = [tok, tok, tok]
    args = [q, f, v]
    if s0 is not None:
        in_specs.append(pl.BlockSpec((ns, 1, 1, N_HEADS, HEAD, HEAD), lambda b, i: (b, 0, s0_dir, 0, 0, 0)))
        args.append(s0)
    if final:
        oprev, gs, gain = final_inputs
        in_specs += [tok, tok, pl.BlockSpec((1, HW), lambda b, i: (0, 0))]
        args += [oprev, gs, gain]
    out_shape = [jax.ShapeDtypeStruct((n_batch, seq, HW), BF16 if final else F32),
                 jax.ShapeDtypeStruct((n_batch, N_HEADS, HEAD, HEAD), F32)]
    out_specs = [tok, pl.BlockSpec((ns, N_HEADS, HEAD, HEAD), lambda b, i: (b, 0, 0, 0))]
    return pl.pallas_call(
        functools.partial(_hgrn_kernel, reverse=reverse, zero_init=s0 is None, final=final, n_seq=ns),
        out_shape=out_shape,
        grid=(n_batch // ns, n_chunks),
        in_specs=in_specs,
        out_specs=out_specs,
        scratch_shapes=[pltpu.VMEM((ns * N_HEADS, HEAD, HEAD), F32)],
        compiler_params=_params("arbitrary", "arbitrary"),
        name="hgrn2_bwd" if reverse else "hgrn2_fwd",
    )(*args)


def _merge_kernel(x_ref, y_ref, o_ref, ga_ref, gb_ref, mod_ref, npost_ref, wf_ref, wh_ref, wo_ref, out_ref):
    ba = _dot(y_ref[...], wf_ref[...])
    bb = _dot(o_ref[...], wh_ref[...])
    merged = ga_ref[...].astype(F32) * ba + gb_ref[...].astype(F32) * bb
    m = _dot(merged.astype(BF16), wo_ref[...])
    out_ref[...] = x_ref[...] + mod_ref[0, 5:6, :] * _rms(m, npost_ref[...])


def _merge(x, y, o, ga, gb, mod, npost, wf, wh, wo, *, tiles_per_mod):
    t = x.shape[0]
    tm = TOKEN_TILE

    def tok(n):
        return pl.BlockSpec((tm, n), lambda i: (i, 0))

    return pl.pallas_call(
        _merge_kernel,
        out_shape=jax.ShapeDtypeStruct((t, D_MODEL), F32),
        grid=(t // tm,),
        in_specs=[
            tok(D_MODEL), tok(FOUR_W), tok(HW), tok(D_MODEL), tok(D_MODEL),
            pl.BlockSpec((1, N_MOD, D_MODEL), lambda i: (i // tiles_per_mod, 0, 0)),
            _const_spec((1, D_MODEL)),
            _const_spec((FOUR_W, D_MODEL)),
            _const_spec((HW, D_MODEL)),
            _const_spec((D_MODEL, D_MODEL)),
        ],
        out_specs=tok(D_MODEL),
        compiler_params=_params("parallel"),
        name="mixer_merge",
    )(x, y, o, ga, gb, mod, npost, wf, wh, wo)


def _trunk_layer(x, mod, w, *, n_batch, seq, grid_rows, s0):
    n_tiles = x.shape[0] // TOKEN_TILE
    tiles_per_mod = n_tiles // mod.shape[0]
    x1 = _ffn(x, mod, w["npre"][0], w["npost"][0], w["ffn_in"][0], w["ffn_out"][0],
              mod_base=0, tiles_per_mod=tiles_per_mod)
    u, q, ff, fb, v, gs, ga, gb = _proj(x1, mod, w["npre"][1], w["lb_logits"], w["w_in"],
                                        tiles_per_mod=tiles_per_mod)
    if grid_rows is None:
        y = _fourier_seq(u, seq)
    else:
        y = _fourier_grid(u, n_batch, grid_rows)
    q, ff, fb, v, gs = (z.reshape(n_batch, seq, HW) for z in (q, ff, fb, v, gs))
    o_f, s_f = _hgrn(q, ff, v, n_batch=n_batch, seq=seq, reverse=False, s0=s0, s0_dir=0)
    o, s_b = _hgrn(q, fb, v, n_batch=n_batch, seq=seq, reverse=True, s0=s0, s0_dir=1,
                   final_inputs=(o_f, gs, w["hgrn_gain"]))
    x2 = _merge(x1, y, o.reshape(n_batch * seq, HW), ga, gb, mod, w["npost"][1], w["w_four"], w["w_hgrn"],
                w["w_out"], tiles_per_mod=tiles_per_mod)
    out = _ffn(x2, mod, w["npre"][2], w["npost"][2], w["ffn_in"][1], w["ffn_out"][1],
               mod_base=6, tiles_per_mod=tiles_per_mod)
    return out, s_f, s_b


def kernel(x_prompt, x_sample, state_hgrn, c, c_ctx, w_mod, b_mod, norm_pre, norm_post, ffn_w_in, ffn_w_out,
           w_in, w_four, hgrn_gain, w_hgrn, w_out, lb_logits):
    batch, seq, d = x_prompt.shape
    dec_batch, dec_seq, _ = x_sample.shape
    depth = w_mod.shape[0]
    assert depth == 1 and d == D_MODEL
    rows = dec_seq // GRID_W
    layer = 0

    cond = jnp.concatenate([c_ctx[None, :], c, jnp.zeros((MOD_ROWS - 1 - dec_batch, d), F32)], axis=0)
    mod = _modulation(cond, w_mod[layer], b_mod[layer]).reshape(MOD_ROWS, N_MOD, d)
    mod_ctx = mod[0:1]
    mod_dec = mod[1:1 + dec_batch]

    w = dict(
        npre=norm_pre[layer].reshape(3, 1, d),
        npost=norm_post[layer].reshape(3, 1, d),
        ffn_in=ffn_w_in[layer].astype(BF16),
        ffn_out=ffn_w_out[layer].astype(BF16),
        w_in=w_in[layer].astype(BF16),
        w_four=w_four[layer].astype(BF16),
        w_hgrn=w_hgrn[layer].astype(BF16),
        w_out=w_out[layer].astype(BF16),
        hgrn_gain=hgrn_gain[layer].reshape(1, HW),
        lb_logits=lb_logits[layer:layer + 2],
    )

    y_prompt, s_f, s_b = _trunk_layer(x_prompt.reshape(batch * seq, d), mod_ctx, w,
                                      n_batch=batch, seq=seq, grid_rows=None, s0=None)
    new_state = jnp.stack([s_f, s_b], axis=1)[:, None].astype(x_prompt.dtype)

    y_sample, _, _ = _trunk_layer(x_sample.reshape(dec_batch * dec_seq, d), mod_dec, w,
                                  n_batch=dec_batch, seq=dec_seq, grid_rows=rows, s0=state_hgrn)
    return (y_prompt.reshape(batch, seq, d), y_sample.reshape(dec_batch, dec_seq, d), new_state)
```

```python
import functools

import jax
import jax.numpy as jnp
import numpy as np
from jax import lax
from jax.experimental import pallas as pl
from jax.experimental.pallas import tpu as pltpu

D_MODEL = 1024
D_FF = 2816
N_MOD = 9
EPS = 1e-6
GRID_W = 64
N_GROUPS = 4
GROUP = 128
FOUR_W = N_GROUPS * GROUP
N_HEADS = 4
HEAD = 128
HW = N_HEADS * HEAD
IN_COLS = FOUR_W + 3 * HW + 2 * HW + 2 * D_MODEL

F32 = jnp.float32
BF16 = jnp.bfloat16

VMEM_LIMIT_BYTES = 56 * 1024 * 1024
FF_CHUNK = 256
TOKEN_TILE = 512
HGRN_CHUNK = 128
HGRN_SEQS = 8
HGRN_STAGE_LAG = 3
MOD_ROWS = 16
MOD_COL_TILE = 1536


def _params(*sem):
    return pltpu.CompilerParams(dimension_semantics=sem, vmem_limit_bytes=VMEM_LIMIT_BYTES)


def _const_spec(shape):
    nd = len(shape)
    return pl.BlockSpec(shape, lambda *_: (0,) * nd, pipeline_mode=pl.Buffered(1))


def _sigmoid(x):
    return 1.0 / (1.0 + jnp.exp(-x))


def _silu(x):
    return x * _sigmoid(x)


def _rms(x, gain):
    return x * lax.rsqrt(jnp.mean(x * x, axis=-1, keepdims=True) + EPS) * gain


def _dot(a, b):
    return jnp.dot(a, b, preferred_element_type=F32)


def _dot_nt(a, b):
    return lax.dot_general(a, b, (((1,), (1,)), ((), ())), preferred_element_type=F32)


def _mod_kernel(cond_ref, w_ref, b_ref, out_ref):
    a = _silu(cond_ref[...]).astype(BF16)
    out_ref[...] = _dot(a, w_ref[...].astype(BF16)) + b_ref[...]


def _modulation(cond, w_mod, b_mod):
    n = w_mod.shape[1]
    return pl.pallas_call(
        _mod_kernel,
        out_shape=jax.ShapeDtypeStruct((MOD_ROWS, n), F32),
        grid=(n // MOD_COL_TILE,),
        in_specs=[
            pl.BlockSpec((MOD_ROWS, D_MODEL), lambda j: (0, 0)),
            pl.BlockSpec((D_MODEL, MOD_COL_TILE), lambda j: (0, j)),
            pl.BlockSpec((1, MOD_COL_TILE), lambda j: (0, j)),
        ],
        out_specs=pl.BlockSpec((MOD_ROWS, MOD_COL_TILE), lambda j: (0, j)),
        compiler_params=_params("parallel"),
        name="modulation",
    )(cond, w_mod, b_mod.reshape(1, n))


def _ffn_kernel(x_ref, mod_ref, npre_ref, npost_ref, win_ref, wout_ref, out_ref, acc_ref, *, mod_base):
    x = x_ref[...]
    sh = mod_ref[0, mod_base:mod_base + 1, :]
    sc = mod_ref[0, mod_base + 1:mod_base + 2, :]
    gm = mod_ref[0, mod_base + 2:mod_base + 3, :]
    h = (_rms(x, npre_ref[...]) * (1.0 + sc) + sh).astype(BF16)
    for j in range(D_FF // FF_CHUNK):
        lo = j * FF_CHUNK
        g = _dot(h, win_ref[:, lo:lo + FF_CHUNK])
        u = _dot(h, win_ref[:, D_FF + lo:D_FF + lo + FF_CHUNK])
        part = _dot((_silu(g) * u).astype(BF16), wout_ref[lo:lo + FF_CHUNK, :])
        if j == 0:
            acc_ref[...] = part
        else:
            acc_ref[...] += part
    out_ref[...] = x + 0.5 * gm * _rms(acc_ref[...], npost_ref[...])


def _ffn(x, mod, npre, npost, win, wout, *, mod_base, tiles_per_mod):
    t = x.shape[0]
    tm = TOKEN_TILE
    return pl.pallas_call(
        functools.partial(_ffn_kernel, mod_base=mod_base),
        out_shape=jax.ShapeDtypeStruct((t, D_MODEL), F32),
        grid=(t // tm,),
        in_specs=[
            pl.BlockSpec((tm, D_MODEL), lambda i: (i, 0)),
            pl.BlockSpec((1, N_MOD, D_MODEL), lambda i: (i // tiles_per_mod, 0, 0)),
            _const_spec((1, D_MODEL)),
            _const_spec((1, D_MODEL)),
            _const_spec((D_MODEL, 2 * D_FF)),
            _const_spec((D_FF, D_MODEL)),
        ],
        out_specs=pl.BlockSpec((tm, D_MODEL), lambda i: (i, 0)),
        scratch_shapes=[pltpu.VMEM((tm, D_MODEL), F32)],
        compiler_params=_params("parallel"),
        name="ffn_half_step",
    )(x, mod, npre, npost, win, wout)


SUBLANES = 8
CHUNK_VREGS = HGRN_CHUNK // SUBLANES


def _chunk_token_of_row(row):
    return (row & (SUBLANES - 1)) * CHUNK_VREGS + (row >> 3)


def _chunk_perm_matrix(transpose):
    c = HGRN_CHUNK
    a = lax.broadcasted_iota(jnp.int32, (c, c), 0)
    b = lax.broadcasted_iota(jnp.int32, (c, c), 1)
    hit = (a == _chunk_token_of_row(b)) if transpose else (b == _chunk_token_of_row(a))
    return jnp.where(hit, 1.0, 0.0).astype(BF16)


def _proj_kernel(x_ref, mod_ref, npre_ref, lbl_ref, win_ref,
                 u_ref, q_ref, ff_ref, fb_ref, v_ref, g_ref, ga_ref, gb_ref):
    x = x_ref[...]
    sh = mod_ref[0, 3:4, :]
    sc = mod_ref[0, 4:5, :]
    h = (_rms(x, npre_ref[...]) * (1.0 + sc) + sh).astype(BF16)
    perm = _chunk_perm_matrix(transpose=False)
    c = HGRN_CHUNK
    hp = jnp.concatenate([_dot(perm, h[j * c:(j + 1) * c]) for j in range(h.shape[0] // c)],
                         axis=0).astype(BF16)

    def cols(lhs, lo, n):
        return _dot(lhs, win_ref[:, lo:lo + n])

    l0 = lbl_ref[0]
    l1 = lbl_ref[1]
    mx = jnp.maximum(l0, l1)
    e0 = jnp.exp(l0 - mx)
    e1 = jnp.exp(l1 - mx)
    lb = e0 / (e0 + e1)

    u_ref[...] = cols(h, 0, FOUR_W).astype(BF16)
    q_ref[...] = _silu(cols(hp, FOUR_W, HW)).astype(BF16)
    ff_ref[...] = lb[0:1, :] + (1.0 - lb[0:1, :]) * _sigmoid(cols(hp, FOUR_W + HW, HW))
    fb_ref[...] = lb[1:2, :] + (1.0 - lb[1:2, :]) * _sigmoid(cols(hp, FOUR_W + 2 * HW, HW))
    v_ref[...] = cols(hp, FOUR_W + 3 * HW, HW).astype(BF16)
    g_ref[...] = _silu(cols(hp, FOUR_W + 4 * HW, HW)).astype(BF16)
    ga_ref[...] = _sigmoid(cols(h, FOUR_W + 5 * HW, D_MODEL)).astype(BF16)
    gb_ref[...] = _sigmoid(cols(h, FOUR_W + 5 * HW + D_MODEL, D_MODEL)).astype(BF16)


def _proj(x, mod, npre, lb_logits, win, *, tiles_per_mod):
    t = x.shape[0]
    tm = TOKEN_TILE

    def tok(n):
        return pl.BlockSpec((tm, n), lambda i: (i, 0))

    shapes = [
        jax.ShapeDtypeStruct((t, FOUR_W), BF16),
        jax.ShapeDtypeStruct((t, HW), BF16),
        jax.ShapeDtypeStruct((t, HW), F32),
        jax.ShapeDtypeStruct((t, HW), F32),
        jax.ShapeDtypeStruct((t, HW), BF16),
        jax.ShapeDtypeStruct((t, HW), BF16),
        jax.ShapeDtypeStruct((t, D_MODEL), BF16),
        jax.ShapeDtypeStruct((t, D_MODEL), BF16),
    ]
    return pl.pallas_call(
        _proj_kernel,
        out_shape=shapes,
        grid=(t // tm,),
        in_specs=[
            tok(D_MODEL),
            pl.BlockSpec((1, N_MOD, D_MODEL), lambda i: (i // tiles_per_mod, 0, 0)),
            _const_spec((1, D_MODEL)),
            _const_spec((2, 2, HW)),
            _const_spec((D_MODEL, IN_COLS)),
        ],
        out_specs=[tok(FOUR_W), tok(HW), tok(HW), tok(HW), tok(HW), tok(HW), tok(D_MODEL), tok(D_MODEL)],
        compiler_params=_params("parallel"),
        name="mixer_in_proj",
    )(x, mod, npre, lb_logits, win)


def _cos_sin(n):
    k = np.arange(n)
    ang = 2.0 * np.pi * ((k[:, None] * k[None, :]) % n) / n
    return np.cos(ang), np.sin(ang)


def _bf16_const(a):
    return jnp.asarray(a, F32).astype(BF16)


def _channel_dft(u):
    def go(cs_ref):
        a_parts, b_parts = [], []
        for g in range(N_GROUPS):
            ab = _dot(u[:, g * GROUP:(g + 1) * GROUP], cs_ref[...])
            a_parts.append(ab[:, :GROUP])
            b_parts.append(ab[:, GROUP:])
        return (jnp.concatenate(a_parts, axis=1).astype(BF16),
                jnp.concatenate(b_parts, axis=1).astype(BF16))
    return go


def _fourier_seq_kernel(u_ref, cs_ref, pos_ref, y_ref, *, seq, n_seq):
    for s in range(n_seq):
        a, b = _channel_dft(u_ref[s * seq:(s + 1) * seq, :])(cs_ref)
        ab = jnp.concatenate([a, b], axis=0)
        y_ref[s * seq:(s + 1) * seq, :] = _dot(pos_ref[...], ab).astype(BF16)


def _fourier_seq(u, seq):
    t = u.shape[0]
    n_seq = 4
    cc, sc = _cos_sin(GROUP)
    cl, sl = _cos_sin(seq)
    cs = _bf16_const(np.concatenate([cc, sc], axis=1) / np.sqrt(GROUP))
    pos = _bf16_const(np.concatenate([cl, -sl], axis=1) / np.sqrt(seq))
    rows = n_seq * seq
    return pl.pallas_call(
        functools.partial(_fourier_seq_kernel, seq=seq, n_seq=n_seq),
        out_shape=jax.ShapeDtypeStruct((t, FOUR_W), BF16),
        grid=(t // rows,),
        in_specs=[
            pl.BlockSpec((rows, FOUR_W), lambda i: (i, 0)),
            _const_spec((GROUP, 2 * GROUP)),
            _const_spec((seq, 2 * seq)),
        ],
        out_specs=pl.BlockSpec((rows, FOUR_W), lambda i: (i, 0)),
        compiler_params=_params("parallel"),
        name="fourier_seq",
    )(u, cs, pos)


W_BLOCK = 16


def _fourier_grid_kernel(u_ref, cs_ref, m1_ref, k2_ref, y_ref, p_ref, q_ref, *, n_rows):
    def row_body(r, carry):
        start = pl.multiple_of(r * GRID_W, GRID_W)
        a, b = _channel_dft(u_ref[pl.ds(start, GRID_W), :])(cs_ref)
        ab = jnp.concatenate([a, b], axis=0)
        pq = _dot(m1_ref[...], ab)
        p_ref[r] = pq[:GRID_W]
        q_ref[r] = pq[GRID_W:]
        return carry

    lax.fori_loop(0, n_rows, row_body, 0)

    for j in range(GRID_W // W_BLOCK):
        ws = slice(j * W_BLOCK, (j + 1) * W_BLOCK)
        pb = p_ref[:, ws, :].reshape(n_rows * W_BLOCK, FOUR_W).astype(BF16)
        qb = q_ref[:, ws, :].reshape(n_rows * W_BLOCK, FOUR_W).astype(BF16)
        yb = _dot(k2_ref[...], jnp.concatenate([pb, qb], axis=0))
        y_ref[0, :, ws, :] = yb.reshape(n_rows, W_BLOCK, FOUR_W).astype(BF16)


def _fourier_grid(u, n_batch, n_rows):
    cc, sc = _cos_sin(GROUP)
    cw, sw = _cos_sin(GRID_W)
    cr, sr = _cos_sin(n_rows)
    cs = _bf16_const(np.concatenate([cc, sc], axis=1) / np.sqrt(GROUP))
    m1 = _bf16_const(np.block([[cw, -sw], [sw, cw]]) / np.sqrt(GRID_W))
    eye = np.eye(W_BLOCK)
    k2 = _bf16_const(np.concatenate([np.kron(cr, eye), -np.kron(sr, eye)], axis=1) / np.sqrt(n_rows))
    seq = n_rows * GRID_W
    nk = n_rows * W_BLOCK
    y = pl.pallas_call(
        functools.partial(_fourier_grid_kernel, n_rows=n_rows),
        out_shape=jax.ShapeDtypeStruct((n_batch, n_rows, GRID_W, FOUR_W), BF16),
        grid=(n_batch,),
        in_specs=[
            pl.BlockSpec((seq, FOUR_W), lambda b: (b, 0)),
            _const_spec((GROUP, 2 * GROUP)),
            _const_spec((2 * GRID_W, 2 * GRID_W)),
            _const_spec((nk, 2 * nk)),
        ],
        out_specs=pl.BlockSpec((1, n_rows, GRID_W, FOUR_W), lambda b: (b, 0, 0, 0)),
        scratch_shapes=[pltpu.VMEM((n_rows, GRID_W, FOUR_W), F32),
                        pltpu.VMEM((n_rows, GRID_W, FOUR_W), F32)],
        compiler_params=_params("parallel"),
        name="fourier_grid",
    )(u, cs, m1, k2)
    return y.reshape(n_batch * seq, FOUR_W)


def _hgrn_kernel(*refs, reverse, zero_init, final, n_seq):
    it = iter(refs)
    q_ref, f_ref, v_ref = next(it), next(it), next(it)
    s0_ref = None if zero_init else next(it)
    if final:
        oprev_ref, gs_ref, gain_ref = next(it), next(it), next(it)
    o_ref, sfin_ref, st_ref = next(it), next(it), next(it)

    i = pl.program_id(1)
    c = HGRN_CHUNK
    nv = CHUNK_VREGS

    @pl.when(i == 0)
    def _():
        for sq in range(n_seq):
            for hd in range(N_HEADS):
                if zero_init:
                    st_ref[sq * N_HEADS + hd] = jnp.zeros((HEAD, HEAD), F32)
                else:
                    st_ref[sq * N_HEADS + hd] = s0_ref[sq, 0, 0, hd].T

    row_a = lax.broadcasted_iota(jnp.int32, (c, c), 0)
    row_b = lax.broadcasted_iota(jnp.int32, (c, c), 1)
    tt = _chunk_token_of_row(row_a)
    ss = _chunk_token_of_row(row_b)
    xor = tt ^ ss
    ordered = (tt < ss) if reverse else (tt > ss)
    vreg_levels = [1 << b for b in range(nv.bit_length() - 1)]
    sub_levels = [1 << b for b in range(SUBLANES.bit_length() - 1)]
    level_mask = {h: ordered & (xor >= h) & (xor < 2 * h) for h in vreg_levels + [nv * g for g in sub_levels]}
    diag_mask = row_a == row_b
    sub = lax.broadcasted_iota(jnp.int32, (SUBLANES, HEAD), 0)
    out_bit = 0 if reverse else 1
    if final:
        unperm = _chunk_perm_matrix(transpose=True)

    def vregs(x):
        return [x[SUBLANES * j:SUBLANES * (j + 1)] for j in range(nv)]

    def operand(xs):
        return jnp.concatenate(xs, axis=0).astype(BF16)

    def times(x, m):
        return m if x is None else x * m

    def lanes(hd):
        return slice(hd * HEAD, (hd + 1) * HEAD)

    def level_stage(sq, hd):
        sl = lanes(hd)
        f = f_ref[sq, :, sl]
        q = q_ref[sq, :, sl].astype(F32)
        vb = v_ref[sq, :, sl]
        k = 1.0 - f
        qv, kv = vregs(q), vregs(k)
        p = vregs(f)
        r = [None] * nv
        tot = vregs(f)
        a = jnp.zeros((c, c), F32)
        for h in vreg_levels:
            is_out = [((x // h) & 1) == out_bit for x in range(nv)]
            qop = operand([qv[x] * p[x] if is_out[x] else qv[x] for x in range(nv)])
            kop = operand([kv[x] if is_out[x] else times(r[x], kv[x]) for x in range(nv)])
            a = jnp.where(level_mask[h], _dot_nt(qop, kop), a)
            new_tot = []
            for b2 in range(len(tot) // 2):
                lo, hi = tot[2 * b2], tot[2 * b2 + 1]
                lo_rows = range(2 * b2 * h, (2 * b2 + 1) * h)
                hi_rows = range((2 * b2 + 1) * h, (2 * b2 + 2) * h)
                if reverse:
                    for x in lo_rows:
                        p[x] = p[x] * hi
                    for x in hi_rows:
                        r[x] = times(r[x], lo)
                else:
                    for x in hi_rows:
                        p[x] = p[x] * lo
                    for x in lo_rows:
                        r[x] = times(r[x], hi)
                new_tot.append(lo * hi)
            tot = new_tot
        t8 = tot[0]
        for g in sub_levels:
            qop = operand([qv[x] * p[x] for x in range(nv)])
            kop = operand([times(r[x], kv[x]) for x in range(nv)])
            a = jnp.where(level_mask[nv * g], _dot_nt(qop, kop), a)
            hi_half = (sub & g) != 0
            w = jnp.where(hi_half, pltpu.roll(t8, g, 0), pltpu.roll(t8, SUBLANES - g, 0))
            grow_p = hi_half != reverse
            mp = jnp.where(grow_p, w, 1.0)
            mr = jnp.where(grow_p, 1.0, w)
            p = [x * mp for x in p]
            r = [times(x, mr) for x in r]
            t8 = t8 * w
        a = jnp.where(diag_mask, jnp.sum(q * k, axis=-1, keepdims=True), a)
        qdec = operand([qv[x] * p[x] for x in range(nv)])
        kdec = operand([kv[x] * r[x] for x in range(nv)])
        return a.astype(BF16), qdec, kdec, t8[0:1, :], vb

    def state_stage(sq, hd, a, qdec, kdec, total, vb):
        slot = sq * N_HEADS + hd
        st = st_ref[slot]
        o = _dot(a, vb) + _dot_nt(qdec, st.astype(BF16))
        vt = vb.astype(F32).T.astype(BF16)
        st_ref[slot] = st * total + _dot(vt, kdec)
        return o

    def output_stage(sq, hd, o):
        sl = lanes(hd)
        if final:
            o = o + oprev_ref[sq, :, sl]
            o = o * lax.rsqrt(jnp.mean(o * o, axis=-1, keepdims=True) + EPS)
            o = o * gain_ref[:, sl] * gs_ref[sq, :, sl].astype(F32)
            o_ref[sq, :, sl] = _dot(unperm, o.astype(BF16)).astype(BF16)
        else:
            o_ref[sq, :, sl] = o

    units = [(sq, hd) for sq in range(n_seq) for hd in range(N_HEADS)]
    scores, outs = {}, {}
    for step in range(len(units) + 2 * HGRN_STAGE_LAG):
        if step < len(units):
            scores[step] = level_stage(*units[step])
        j = step - HGRN_STAGE_LAG
        if 0 <= j < len(units):
            outs[j] = state_stage(*units[j], *scores.pop(j))
        j = step - 2 * HGRN_STAGE_LAG
        if 0 <= j < len(units):
            output_stage(*units[j], outs.pop(j))

    @pl.when(i == pl.num_programs(1) - 1)
    def _():
        for sq in range(n_seq):
            for hd in range(N_HEADS):
                sfin_ref[sq, hd] = st_ref[sq * N_HEADS + hd].T


def _hgrn(q, f, v, *, n_batch, seq, reverse, s0=None, s0_dir=0, final_inputs=None):
    c = HGRN_CHUNK
    ns = HGRN_SEQS
    n_chunks = seq // c
    final = final_inputs is not None

    def tok_map(b, i):
        return (b, (n_chunks - 1 - i) if reverse else i, 0)

    tok = pl.BlockSpec((ns, c, HW), tok_map)
    in_specs = [tok, tok, tok]
    args = [q, f, v]
    if s0 is not None:
        in_specs.append(pl.BlockSpec((ns, 1, 1, N_HEADS, HEAD, HEAD), lambda b, i: (b, 0, s0_dir, 0, 0, 0)))
        args.append(s0)
    if final:
        oprev, gs, gain = final_inputs
        in_specs += [tok, tok, pl.BlockSpec((1, HW), lambda b, i: (0, 0))]
        args += [oprev, gs, gain]
    out_shape = [jax.ShapeDtypeStruct((n_batch, seq, HW), BF16 if final else F32),
                 jax.ShapeDtypeStruct((n_batch, N_HEADS, HEAD, HEAD), F32)]
    out_specs = [tok, pl.BlockSpec((ns, N_HEADS, HEAD, HEAD), lambda b, i: (b, 0, 0, 0))]
    return pl.pallas_call(
        functools.partial(_hgrn_kernel, reverse=reverse, zero_init=s0 is None, final=final, n_seq=ns),
        out_shape=out_shape,
        grid=(n_batch // ns, n_chunks),
        in_specs=in_specs,
        out_specs=out_specs,
        scratch_shapes=[pltpu.VMEM((ns * N_HEADS, HEAD, HEAD), F32)],
        compiler_params=_params("arbitrary", "arbitrary"),
        name="hgrn2_bwd" if reverse else "hgrn2_fwd",
    )(*args)


def _merge_kernel(x_ref, y_ref, o_ref, ga_ref, gb_ref, mod_ref, npost_ref, wf_ref, wh_ref, wo_ref, out_ref):
    ba = _dot(y_ref[...], wf_ref[...])
    bb = _dot(o_ref[...], wh_ref[...])
    merged = ga_ref[...].astype(F32) * ba + gb_ref[...].astype(F32) * bb
    m = _dot(merged.astype(BF16), wo_ref[...])
    out_ref[...] = x_ref[...] + mod_ref[0, 5:6, :] * _rms(m, npost_ref[...])


def _merge(x, y, o, ga, gb, mod, npost, wf, wh, wo, *, tiles_per_mod):
    t = x.shape[0]
    tm = TOKEN_TILE

    def tok(n):
        return pl.BlockSpec((tm, n), lambda i: (i, 0))

    return pl.pallas_call(
        _merge_kernel,
        out_shape=jax.ShapeDtypeStruct((t, D_MODEL), F32),
        grid=(t // tm,),
        in_specs=[
            tok(D_MODEL), tok(FOUR_W), tok(HW), tok(D_MODEL), tok(D_MODEL),
            pl.BlockSpec((1, N_MOD, D_MODEL), lambda i: (i // tiles_per_mod, 0, 0)),
            _const_spec((1, D_MODEL)),
            _const_spec((FOUR_W, D_MODEL)),
            _const_spec((HW, D_MODEL)),
            _const_spec((D_MODEL, D_MODEL)),
        ],
        out_specs=tok(D_MODEL),
        compiler_params=_params("parallel"),
        name="mixer_merge",
    )(x, y, o, ga, gb, mod, npost, wf, wh, wo)


def _trunk_layer(x, mod, w, *, n_batch, seq, grid_rows, s0):
    n_tiles = x.shape[0] // TOKEN_TILE
    tiles_per_mod = n_tiles // mod.shape[0]
    x1 = _ffn(x, mod, w["npre"][0], w["npost"][0], w["ffn_in"][0], w["ffn_out"][0],
              mod_base=0, tiles_per_mod=tiles_per_mod)
    u, q, ff, fb, v, gs, ga, gb = _proj(x1, mod, w["npre"][1], w["lb_logits"], w["w_in"],
                                        tiles_per_mod=tiles_per_mod)
    if grid_rows is None:
        y = _fourier_seq(u, seq)
    else:
        y = _fourier_grid(u, n_batch, grid_rows)
    q, ff, fb, v, gs = (z.reshape(n_batch, seq, HW) for z in (q, ff, fb, v, gs))
    o_f, s_f = _hgrn(q, ff, v, n_batch=n_batch, seq=seq, reverse=False, s0=s0, s0_dir=0)
    o, s_b = _hgrn(q, fb, v, n_batch=n_batch, seq=seq, reverse=True, s0=s0, s0_dir=1,
                   final_inputs=(o_f, gs, w["hgrn_gain"]))
    x2 = _merge(x1, y, o.reshape(n_batch * seq, HW), ga, gb, mod, w["npost"][1], w["w_four"], w["w_hgrn"],
                w["w_out"], tiles_per_mod=tiles_per_mod)
    out = _ffn(x2, mod, w["npre"][2], w["npost"][2], w["ffn_in"][1], w["ffn_out"][1],
               mod_base=6, tiles_per_mod=tiles_per_mod)
    return out, s_f, s_b


def kernel(x_prompt, x_sample, state_hgrn, c, c_ctx, w_mod, b_mod, norm_pre, norm_post, ffn_w_in, ffn_w_out,
           w_in, w_four, hgrn_gain, w_hgrn, w_out, lb_logits):
    batch, seq, d = x_prompt.shape
    dec_batch, dec_seq, _ = x_sample.shape
    depth = w_mod.shape[0]
    assert depth == 1 and d == D_MODEL
    rows = dec_seq // GRID_W
    layer = 0

    cond = jnp.concatenate([c_ctx[None, :], c, jnp.zeros((MOD_ROWS - 1 - dec_batch, d), F32)], axis=0)
    mod = _modulation(cond, w_mod[layer], b_mod[layer]).reshape(MOD_ROWS, N_MOD, d)
    mod_ctx = mod[0:1]
    mod_dec = mod[1:1 + dec_batch]

    w = dict(
        npre=norm_pre[layer].reshape(3, 1, d),
        npost=norm_post[layer].reshape(3, 1, d),
        ffn_in=ffn_w_in[layer].astype(BF16),
        ffn_out=ffn_w_out[layer].astype(BF16),
        w_in=w_in[layer].astype(BF16),
        w_four=w_four[layer].astype(BF16),
        w_hgrn=w_hgrn[layer].astype(BF16),
        w_out=w_out[layer].astype(BF16),
        hgrn_gain=hgrn_gain[layer].reshape(1, HW),
        lb_logits=lb_logits[layer:layer + 2],
    )

    y_prompt, s_f, s_b = _trunk_layer(x_prompt.reshape(batch * seq, d), mod_ctx, w,
                                      n_batch=batch, seq=seq, grid_rows=None, s0=None)
    new_state = jnp.stack([s_f, s_b], axis=1)[:, None].astype(x_prompt.dtype)

    y_sample, _, _ = _trunk_layer(x_sample.reshape(dec_batch * dec_seq, d), mod_dec, w,
                                  n_batch=dec_batch, seq=dec_seq, grid_rows=rows, s0=state_hgrn)
    return (y_prompt.reshape(batch, seq, d), y_sample.reshape(dec_batch, dec_seq, d), new_state)
```

```python
import functools

import jax
import jax.numpy as jnp
import numpy as np
from jax import lax
from jax.experimental import pallas as pl
from jax.experimental.pallas import tpu as pltpu

D_MODEL = 1024
D_FF = 2816
N_MOD = 9
EPS = 1e-6
GRID_W = 64
N_GROUPS = 4
GROUP = 128
FOUR_W = N_GROUPS * GROUP
N_HEADS = 4
HEAD = 128
HW = N_HEADS * HEAD
MIX_COLS = FOUR_W + 3 * HW + 2 * HW
GATE_COLS = 2 * D_MODEL

F32 = jnp.float32
BF16 = jnp.bfloat16

VMEM_LIMIT_BYTES = 56 * 1024 * 1024
FF_CHUNK = 256
TOKEN_TILE = 512
HGRN_CHUNK = 128
HGRN_SEQS = 8
HGRN_STAGE_LAG = 3
MOD_ROWS = 16
MOD_COL_TILE = 1536


def _params(*sem):
    return pltpu.CompilerParams(dimension_semantics=sem, vmem_limit_bytes=VMEM_LIMIT_BYTES)


def _const_spec(shape):
    nd = len(shape)
    return pl.BlockSpec(shape, lambda *_: (0,) * nd, pipeline_mode=pl.Buffered(1))


def _sigmoid(x):
    return 1.0 / (1.0 + jnp.exp(-x))


def _silu(x):
    return x * _sigmoid(x)


def _rms(x, gain):
    return x * lax.rsqrt(jnp.mean(x * x, axis=-1, keepdims=True) + EPS) * gain


def _dot(a, b):
    return jnp.dot(a, b, preferred_element_type=F32)


def _dot_nt(a, b):
    return lax.dot_general(a, b, (((1,), (1,)), ((), ())), preferred_element_type=F32)


def _mod_kernel(cond_ref, w_ref, b_ref, out_ref):
    a = _silu(cond_ref[...]).astype(BF16)
    out_ref[...] = _dot(a, w_ref[...].astype(BF16)) + b_ref[...]


def _modulation(cond, w_mod, b_mod):
    n = w_mod.shape[1]
    return pl.pallas_call(
        _mod_kernel,
        out_shape=jax.ShapeDtypeStruct((MOD_ROWS, n), F32),
        grid=(n // MOD_COL_TILE,),
        in_specs=[
            pl.BlockSpec((MOD_ROWS, D_MODEL), lambda j: (0, 0)),
            pl.BlockSpec((D_MODEL, MOD_COL_TILE), lambda j: (0, j)),
            pl.BlockSpec((1, MOD_COL_TILE), lambda j: (0, j)),
        ],
        out_specs=pl.BlockSpec((MOD_ROWS, MOD_COL_TILE), lambda j: (0, j)),
        compiler_params=_params("parallel"),
        name="modulation",
    )(cond, w_mod, b_mod.reshape(1, n))


def _ffn_kernel(x_ref, mod_ref, npre_ref, npost_ref, win_ref, wout_ref, out_ref, acc_ref, *, mod_base):
    x = x_ref[...]
    sh = mod_ref[0, mod_base:mod_base + 1, :]
    sc = mod_ref[0, mod_base + 1:mod_base + 2, :]
    gm = mod_ref[0, mod_base + 2:mod_base + 3, :]
    h = (_rms(x, npre_ref[...]) * (1.0 + sc) + sh).astype(BF16)
    for j in range(D_FF // FF_CHUNK):
        lo = j * FF_CHUNK
        g = _dot(h, win_ref[:, lo:lo + FF_CHUNK])
        u = _dot(h, win_ref[:, D_FF + lo:D_FF + lo + FF_CHUNK])
        part = _dot((_silu(g) * u).astype(BF16), wout_ref[lo:lo + FF_CHUNK, :])
        if j == 0:
            acc_ref[...] = part
        else:
            acc_ref[...] += part
    out_ref[...] = x + 0.5 * gm * _rms(acc_ref[...], npost_ref[...])


def _ffn(x, mod, npre, npost, win, wout, *, mod_base, tiles_per_mod):
    t = x.shape[0]
    tm = TOKEN_TILE
    return pl.pallas_call(
        functools.partial(_ffn_kernel, mod_base=mod_base),
        out_shape=jax.ShapeDtypeStruct((t, D_MODEL), F32),
        grid=(t // tm,),
        in_specs=[
            pl.BlockSpec((tm, D_MODEL), lambda i: (i, 0)),
            pl.BlockSpec((1, N_MOD, D_MODEL), lambda i: (i // tiles_per_mod, 0, 0)),
            _const_spec((1, D_MODEL)),
            _const_spec((1, D_MODEL)),
            _const_spec((D_MODEL, 2 * D_FF)),
            _const_spec((D_FF, D_MODEL)),
        ],
        out_specs=pl.BlockSpec((tm, D_MODEL), lambda i: (i, 0)),
        scratch_shapes=[pltpu.VMEM((tm, D_MODEL), F32)],
        compiler_params=_params("parallel"),
        name="ffn_half_step",
    )(x, mod, npre, npost, win, wout)


SUBLANES = 8
CHUNK_VREGS = HGRN_CHUNK // SUBLANES


def _chunk_token_of_row(row):
    return (row & (SUBLANES - 1)) * CHUNK_VREGS + (row >> 3)


def _chunk_perm_matrix(transpose):
    c = HGRN_CHUNK
    a = lax.broadcasted_iota(jnp.int32, (c, c), 0)
    b = lax.broadcasted_iota(jnp.int32, (c, c), 1)
    hit = (a == _chunk_token_of_row(b)) if transpose else (b == _chunk_token_of_row(a))
    return jnp.where(hit, 1.0, 0.0).astype(BF16)


def _mixer_input(x, mod_ref, npre_ref):
    sh = mod_ref[0, 3:4, :]
    sc = mod_ref[0, 4:5, :]
    return (_rms(x, npre_ref[...]) * (1.0 + sc) + sh).astype(BF16)


def _proj_kernel(x_ref, mod_ref, npre_ref, lbl_ref, win_ref,
                 u_ref, q_ref, ff_ref, fb_ref, v_ref, g_ref):
    h = _mixer_input(x_ref[...], mod_ref, npre_ref)
    perm = _chunk_perm_matrix(transpose=False)
    c = HGRN_CHUNK
    hp = jnp.concatenate([_dot(perm, h[j * c:(j + 1) * c]) for j in range(h.shape[0] // c)],
                         axis=0).astype(BF16)

    def cols(lhs, lo, n):
        return _dot(lhs, win_ref[:, lo:lo + n])

    l0 = lbl_ref[0]
    l1 = lbl_ref[1]
    mx = jnp.maximum(l0, l1)
    e0 = jnp.exp(l0 - mx)
    e1 = jnp.exp(l1 - mx)
    lb = e0 / (e0 + e1)

    u_ref[...] = cols(h, 0, FOUR_W).astype(BF16)
    q_ref[...] = _silu(cols(hp, FOUR_W, HW)).astype(BF16)
    ff_ref[...] = lb[0:1, :] + (1.0 - lb[0:1, :]) * _sigmoid(cols(hp, FOUR_W + HW, HW))
    fb_ref[...] = lb[1:2, :] + (1.0 - lb[1:2, :]) * _sigmoid(cols(hp, FOUR_W + 2 * HW, HW))
    v_ref[...] = cols(hp, FOUR_W + 3 * HW, HW).astype(BF16)
    g_ref[...] = _silu(cols(hp, FOUR_W + 4 * HW, HW)).astype(BF16)


def _proj(x, mod, npre, lb_logits, win, *, tiles_per_mod):
    t = x.shape[0]
    tm = TOKEN_TILE

    def tok(n):
        return pl.BlockSpec((tm, n), lambda i: (i, 0))

    shapes = [
        jax.ShapeDtypeStruct((t, FOUR_W), BF16),
        jax.ShapeDtypeStruct((t, HW), BF16),
        jax.ShapeDtypeStruct((t, HW), F32),
        jax.ShapeDtypeStruct((t, HW), F32),
        jax.ShapeDtypeStruct((t, HW), BF16),
        jax.ShapeDtypeStruct((t, HW), BF16),
    ]
    return pl.pallas_call(
        _proj_kernel,
        out_shape=shapes,
        grid=(t // tm,),
        in_specs=[
            tok(D_MODEL),
            pl.BlockSpec((1, N_MOD, D_MODEL), lambda i: (i // tiles_per_mod, 0, 0)),
            _const_spec((1, D_MODEL)),
            _const_spec((2, 2, HW)),
            _const_spec((D_MODEL, MIX_COLS)),
        ],
        out_specs=[tok(FOUR_W), tok(HW), tok(HW), tok(HW), tok(HW), tok(HW)],
        compiler_params=_params("parallel"),
        name="mixer_in_proj",
    )(x, mod, npre, lb_logits, win)


def _cos_sin(n):
    k = np.arange(n)
    ang = 2.0 * np.pi * ((k[:, None] * k[None, :]) % n) / n
    return np.cos(ang), np.sin(ang)


def _bf16_const(a):
    return jnp.asarray(a, F32).astype(BF16)


def _channel_dft(u):
    def go(cs_ref):
        a_parts, b_parts = [], []
        for g in range(N_GROUPS):
            ab = _dot(u[:, g * GROUP:(g + 1) * GROUP], cs_ref[...])
            a_parts.append(ab[:, :GROUP])
            b_parts.append(ab[:, GROUP:])
        return (jnp.concatenate(a_parts, axis=1).astype(BF16),
                jnp.concatenate(b_parts, axis=1).astype(BF16))
    return go


def _fourier_seq_kernel(u_ref, cs_ref, pos_ref, y_ref, *, seq, n_seq):
    for s in range(n_seq):
        a, b = _channel_dft(u_ref[s * seq:(s + 1) * seq, :])(cs_ref)
        ab = jnp.concatenate([a, b], axis=0)
        y_ref[s * seq:(s + 1) * seq, :] = _dot(pos_ref[...], ab).astype(BF16)


def _fourier_seq(u, seq):
    t = u.shape[0]
    n_seq = 4
    cc, sc = _cos_sin(GROUP)
    cl, sl = _cos_sin(seq)
    cs = _bf16_const(np.concatenate([cc, sc], axis=1) / np.sqrt(GROUP))
    pos = _bf16_const(np.concatenate([cl, -sl], axis=1) / np.sqrt(seq))
    rows = n_seq * seq
    return pl.pallas_call(
        functools.partial(_fourier_seq_kernel, seq=seq, n_seq=n_seq),
        out_shape=jax.ShapeDtypeStruct((t, FOUR_W), BF16),
        grid=(t // rows,),
        in_specs=[
            pl.BlockSpec((rows, FOUR_W), lambda i: (i, 0)),
            _const_spec((GROUP, 2 * GROUP)),
            _const_spec((seq, 2 * seq)),
        ],
        out_specs=pl.BlockSpec((rows, FOUR_W), lambda i: (i, 0)),
        compiler_params=_params("parallel"),
        name="fourier_seq",
    )(u, cs, pos)


W_BLOCK = 8
CHANNEL_DFT_ROWS = 512


def _fourier_grid_kernel(u_ref, cs_ref, m1_ref, k2_ref, y_ref, ab_ref, p_ref, q_ref, *, n_rows):
    seq = n_rows * GRID_W
    for blk in range(seq // CHANNEL_DFT_ROWS):
        rows = slice(blk * CHANNEL_DFT_ROWS, (blk + 1) * CHANNEL_DFT_ROWS)
        a, b = _channel_dft(u_ref[rows, :])(cs_ref)
        ab_ref[0, rows, :] = a
        ab_ref[1, rows, :] = b

    def row_body(r, carry):
        start = pl.multiple_of(r * GRID_W, GRID_W)
        ab = jnp.concatenate([ab_ref[0, pl.ds(start, GRID_W), :], ab_ref[1, pl.ds(start, GRID_W), :]], axis=0)
        pq = _dot(m1_ref[...], ab)
        p_ref[r] = pq[:GRID_W]
        q_ref[r] = pq[GRID_W:]
        return carry

    lax.fori_loop(0, n_rows, row_body, 0, unroll=2)

    for j in range(GRID_W // W_BLOCK):
        ws = slice(j * W_BLOCK, (j + 1) * W_BLOCK)
        pb = p_ref[:, ws, :].reshape(n_rows * W_BLOCK, FOUR_W).astype(BF16)
        qb = q_ref[:, ws, :].reshape(n_rows * W_BLOCK, FOUR_W).astype(BF16)
        yb = _dot(k2_ref[...], jnp.concatenate([pb, qb], axis=0))
        p_ref[:, ws, :] = yb.reshape(n_rows, W_BLOCK, FOUR_W)
    for r0 in range(0, n_rows, SUBLANES):
        y_ref[0, r0:r0 + SUBLANES] = p_ref[r0:r0 + SUBLANES].astype(BF16)


def _fourier_grid(u, n_batch, n_rows):
    cc, sc = _cos_sin(GROUP)
    cw, sw = _cos_sin(GRID_W)
    cr, sr = _cos_sin(n_rows)
    cs = _bf16_const(np.concatenate([cc, sc], axis=1) / np.sqrt(GROUP))
    m1 = _bf16_const(np.block([[cw, -sw], [sw, cw]]) / np.sqrt(GRID_W))
    eye = np.eye(W_BLOCK)
    k2 = _bf16_const(np.concatenate([np.kron(cr, eye), -np.kron(sr, eye)], axis=1) / np.sqrt(n_rows))
    seq = n_rows * GRID_W
    nk = n_rows * W_BLOCK
    y = pl.pallas_call(
        functools.partial(_fourier_grid_kernel, n_rows=n_rows),
        out_shape=jax.ShapeDtypeStruct((n_batch, n_rows, GRID_W, FOUR_W), BF16),
        grid=(n_batch,),
        in_specs=[
            pl.BlockSpec((seq, FOUR_W), lambda b: (b, 0)),
            _const_spec((GROUP, 2 * GROUP)),
            _const_spec((2 * GRID_W, 2 * GRID_W)),
            _const_spec((nk, 2 * nk)),
        ],
        out_specs=pl.BlockSpec((1, n_rows, GRID_W, FOUR_W), lambda b: (b, 0, 0, 0)),
        scratch_shapes=[pltpu.VMEM((2, seq, FOUR_W), BF16),
                        pltpu.VMEM((n_rows, GRID_W, FOUR_W), F32),
                        pltpu.VMEM((n_rows, GRID_W, FOUR_W), F32)],
        compiler_params=_params("parallel"),
        name="fourier_grid",
    )(u, cs, m1, k2)
    return y.reshape(n_batch * seq, FOUR_W)


def _hgrn_kernel(*refs, reverse, zero_init, final, n_seq):
    it = iter(refs)
    q_ref, f_ref, v_ref = next(it), next(it), next(it)
    s0_ref = None if zero_init else next(it)
    if final:
        oprev_ref, gs_ref, gain_ref = next(it), next(it), next(it)
    o_ref, sfin_ref, st_ref = next(it), next(it), next(it)

    i = pl.program_id(1)
    c = HGRN_CHUNK
    nv = CHUNK_VREGS

    @pl.when(i == 0)
    def _():
        for sq in range(n_seq):
            for hd in range(N_HEADS):
                if zero_init:
                    st_ref[sq * N_HEADS + hd] = jnp.zeros((HEAD, HEAD), F32)
                else:
                    st_ref[sq * N_HEADS + hd] = s0_ref[sq, 0, 0, hd].T

    row_a = lax.broadcasted_iota(jnp.int32, (c, c), 0)
    row_b = lax.broadcasted_iota(jnp.int32, (c, c), 1)
    tt = _chunk_token_of_row(row_a)
    ss = _chunk_token_of_row(row_b)
    xor = tt ^ ss
    ordered = (tt < ss) if reverse else (tt > ss)
    vreg_levels = [1 << b for b in range(nv.bit_length() - 1)]
    sub_levels = [1 << b for b in range(SUBLANES.bit_length() - 1)]
    level_mask = {h: ordered & (xor >= h) & (xor < 2 * h) for h in vreg_levels + [nv * g for g in sub_levels]}
    diag_mask = row_a == row_b
    sub = lax.broadcasted_iota(jnp.int32, (SUBLANES, HEAD), 0)
    out_bit = 0 if reverse else 1
    if final:
        unperm = _chunk_perm_matrix(transpose=True)

    def vregs(x):
        return [x[SUBLANES * j:SUBLANES * (j + 1)] for j in range(nv)]

    def operand(xs):
        return jnp.concatenate(xs, axis=0).astype(BF16)

    def times(x, m):
        return m if x is None else x * m

    def lanes(hd):
        return slice(hd * HEAD, (hd + 1) * HEAD)

    def level_stage(sq, hd):
        sl = lanes(hd)
        f = f_ref[sq, :, sl]
        q = q_ref[sq, :, sl].astype(F32)
        vb = v_ref[sq, :, sl]
        k = 1.0 - f
        qv, kv = vregs(q), vregs(k)
        p = vregs(f)
        r = [None] * nv
        tot = vregs(f)
        a = jnp.zeros((c, c), F32)
        for h in vreg_levels:
            is_out = [((x // h) & 1) == out_bit for x in range(nv)]
            qop = operand([qv[x] * p[x] if is_out[x] else qv[x] for x in range(nv)])
            kop = operand([kv[x] if is_out[x] else times(r[x], kv[x]) for x in range(nv)])
            a = jnp.where(level_mask[h], _dot_nt(qop, kop), a)
            new_tot = []
            for b2 in range(len(tot) // 2):
                lo, hi = tot[2 * b2], tot[2 * b2 + 1]
                lo_rows = range(2 * b2 * h, (2 * b2 + 1) * h)
                hi_rows = range((2 * b2 + 1) * h, (2 * b2 + 2) * h)
                if reverse:
                    for x in lo_rows:
                        p[x] = p[x] * hi
                    for x in hi_rows:
                        r[x] = times(r[x], lo)
                else:
                    for x in hi_rows:
                        p[x] = p[x] * lo
                    for x in lo_rows:
                        r[x] = times(r[x], hi)
                new_tot.append(lo * hi)
            tot = new_tot
        t8 = tot[0]
        for g in sub_levels:
            qop = operand([qv[x] * p[x] for x in range(nv)])
            kop = operand([times(r[x], kv[x]) for x in range(nv)])
            a = jnp.where(level_mask[nv * g], _dot_nt(qop, kop), a)
            hi_half = (sub & g) != 0
            w = jnp.where(hi_half, pltpu.roll(t8, g, 0), pltpu.roll(t8, SUBLANES - g, 0))
            grow_p = hi_half != reverse
            mp = jnp.where(grow_p, w, 1.0)
            mr = jnp.where(grow_p, 1.0, w)
            p = [x * mp for x in p]
            r = [times(x, mr) for x in r]
            t8 = t8 * w
        a = jnp.where(diag_mask, jnp.sum(q * k, axis=-1, keepdims=True), a)
        qdec = operand([qv[x] * p[x] for x in range(nv)])
        kdec = operand([kv[x] * r[x] for x in range(nv)])
        return a.astype(BF16), qdec, kdec, t8[0:1, :], vb

    def state_stage(sq, hd, a, qdec, kdec, total, vb):
        slot = sq * N_HEADS + hd
        st = st_ref[slot]
        o = _dot(a, vb) + _dot_nt(qdec, st.astype(BF16))
        vt = vb.astype(F32).T.astype(BF16)
        st_ref[slot] = st * total + _dot(vt, kdec)
        return o

    def output_stage(sq, hd, o):
        sl = lanes(hd)
        if final:
            o = o + oprev_ref[sq, :, sl]
            o = o * lax.rsqrt(jnp.mean(o * o, axis=-1, keepdims=True) + EPS)
            o = o * gain_ref[:, sl] * gs_ref[sq, :, sl].astype(F32)
            o_ref[sq, :, sl] = _dot(unperm, o.astype(BF16)).astype(BF16)
        else:
            o_ref[sq, :, sl] = o

    units = [(sq, hd) for sq in range(n_seq) for hd in range(N_HEADS)]
    scores, outs = {}, {}
    for step in range(len(units) + 2 * HGRN_STAGE_LAG):
        if step < len(units):
            scores[step] = level_stage(*units[step])
        j = step - HGRN_STAGE_LAG
        if 0 <= j < len(units):
            outs[j] = state_stage(*units[j], *scores.pop(j))
        j = step - 2 * HGRN_STAGE_LAG
        if 0 <= j < len(units):
            output_stage(*units[j], outs.pop(j))

    @pl.when(i == pl.num_programs(1) - 1)
    def _():
        for sq in range(n_seq):
            for hd in range(N_HEADS):
                sfin_ref[sq, hd] = st_ref[sq * N_HEADS + hd].T


def _hgrn(q, f, v, *, n_batch, seq, reverse, s0=None, s0_dir=0, final_inputs=None):
    c = HGRN_CHUNK
    ns = HGRN_SEQS
    n_chunks = seq // c
    final = final_inputs is not None

    def tok_map(b, i):
        return (b, (n_chunks - 1 - i) if reverse else i, 0)

    tok = pl.BlockSpec((ns, c, HW), tok_map)
    in_specs = [tok, tok, tok]
    args = [q, f, v]
    if s0 is not None:
        in_specs.append(pl.BlockSpec((ns, 1, 1, N_HEADS, HEAD, HEAD), lambda b, i: (b, 0, s0_dir, 0, 0, 0)))
        args.append(s0)
    if final:
        oprev, gs, gain = final_inputs
        in_specs += [tok, tok, pl.BlockSpec((1, HW), lambda b, i: (0, 0))]
        args += [oprev, gs, gain]
    out_shape = [jax.ShapeDtypeStruct((n_batch, seq, HW), BF16 if final else F32),
                 jax.ShapeDtypeStruct((n_batch, N_HEADS, HEAD, HEAD), F32)]
    out_specs = [tok, pl.BlockSpec((ns, N_HEADS, HEAD, HEAD), lambda b, i: (b, 0, 0, 0))]
    return pl.pallas_call(
        functools.partial(_hgrn_kernel, reverse=reverse, zero_init=s0 is None, final=final, n_seq=ns),
        out_shape=out_shape,
        grid=(n_batch // ns, n_chunks),
        in_specs=in_specs,
        out_specs=out_specs,
        scratch_shapes=[pltpu.VMEM((ns * N_HEADS, HEAD, HEAD), F32)],
        compiler_params=_params("arbitrary", "arbitrary"),
        name="hgrn2_bwd" if reverse else "hgrn2_fwd",
    )(*args)


def _merge_kernel(x_ref, y_ref, o_ref, mod_ref, npre_ref, npost_ref, wg_ref, wf_ref, wh_ref, wo_ref, out_ref):
    x = x_ref[...]
    h = _mixer_input(x, mod_ref, npre_ref)
    merged = (_sigmoid(_dot(h, wg_ref[:, :D_MODEL])) * _dot(y_ref[...], wf_ref[...])
              + _sigmoid(_dot(h, wg_ref[:, D_MODEL:])) * _dot(o_ref[...], wh_ref[...]))
    m = _dot(merged.astype(BF16), wo_ref[...])
    out_ref[...] = x + mod_ref[0, 5:6, :] * _rms(m, npost_ref[...])


def _merge(x, y, o, mod, npre, npost, wg, wf, wh, wo, *, tiles_per_mod):
    t = x.shape[0]
    tm = TOKEN_TILE

    def tok(n):
        return pl.BlockSpec((tm, n), lambda i: (i, 0))

    return pl.pallas_call(
        _merge_kernel,
        out_shape=jax.ShapeDtypeStruct((t, D_MODEL), F32),
        grid=(t // tm,),
        in_specs=[
            tok(D_MODEL), tok(FOUR_W), tok(HW),
            pl.BlockSpec((1, N_MOD, D_MODEL), lambda i: (i // tiles_per_mod, 0, 0)),
            _const_spec((1, D_MODEL)),
            _const_spec((1, D_MODEL)),
            _const_spec((D_MODEL, GATE_COLS)),
            _const_spec((FOUR_W, D_MODEL)),
            _const_spec((HW, D_MODEL)),
            _const_spec((D_MODEL, D_MODEL)),
        ],
        out_specs=tok(D_MODEL),
        compiler_params=_params("parallel"),
        name="mixer_merge",
    )(x, y, o, mod, npre, npost, wg, wf, wh, wo)


def _trunk_layer(x, mod, w, *, n_batch, seq, grid_rows, s0):
    n_tiles = x.shape[0] // TOKEN_TILE
    tiles_per_mod = n_tiles // mod.shape[0]
    x1 = _ffn(x, mod, w["npre"][0], w["npost"][0], w["ffn_in"][0], w["ffn_out"][0],
              mod_base=0, tiles_per_mod=tiles_per_mod)
    u, q, ff, fb, v, gs = _proj(x1, mod, w["npre"][1], w["lb_logits"], w["w_mix"], tiles_per_mod=tiles_per_mod)
    if grid_rows is None:
        y = _fourier_seq(u, seq)
    else:
        y = _fourier_grid(u, n_batch, grid_rows)
    q, ff, fb, v, gs = (z.reshape(n_batch, seq, HW) for z in (q, ff, fb, v, gs))
    o_f, s_f = _hgrn(q, ff, v, n_batch=n_batch, seq=seq, reverse=False, s0=s0, s0_dir=0)
    o, s_b = _hgrn(q, fb, v, n_batch=n_batch, seq=seq, reverse=True, s0=s0, s0_dir=1,
                   final_inputs=(o_f, gs, w["hgrn_gain"]))
    x2 = _merge(x1, y, o.reshape(n_batch * seq, HW), mod, w["npre"][1], w["npost"][1], w["w_gate"],
                w["w_four"], w["w_hgrn"], w["w_out"], tiles_per_mod=tiles_per_mod)
    out = _ffn(x2, mod, w["npre"][2], w["npost"][2], w["ffn_in"][1], w["ffn_out"][1],
               mod_base=6, tiles_per_mod=tiles_per_mod)
    return out, s_f, s_b


def kernel(x_prompt, x_sample, state_hgrn, c, c_ctx, w_mod, b_mod, norm_pre, norm_post, ffn_w_in, ffn_w_out,
           w_in, w_four, hgrn_gain, w_hgrn, w_out, lb_logits):
    batch, seq, d = x_prompt.shape
    dec_batch, dec_seq, _ = x_sample.shape
    depth = w_mod.shape[0]
    assert depth == 1 and d == D_MODEL
    rows = dec_seq // GRID_W
    layer = 0

    cond = jnp.concatenate([c_ctx[None, :], c, jnp.zeros((MOD_ROWS - 1 - dec_batch, d), F32)], axis=0)
    mod = _modulation(cond, w_mod[layer], b_mod[layer]).reshape(MOD_ROWS, N_MOD, d)
    mod_ctx = mod[0:1]
    mod_dec = mod[1:1 + dec_batch]

    w = dict(
        npre=norm_pre[layer].reshape(3, 1, d),
        npost=norm_post[layer].reshape(3, 1, d),
        ffn_in=ffn_w_in[layer].astype(BF16),
        ffn_out=ffn_w_out[layer].astype(BF16),
        w_mix=w_in[layer, :, :MIX_COLS].astype(BF16),
        w_gate=w_in[layer, :, MIX_COLS:].astype(BF16),
        w_four=w_four[layer].astype(BF16),
        w_hgrn=w_hgrn[layer].astype(BF16),
        w_out=w_out[layer].astype(BF16),
        hgrn_gain=hgrn_gain[layer].reshape(1, HW),
        lb_logits=lb_logits[layer:layer + 2],
    )

    y_prompt, s_f, s_b = _trunk_layer(x_prompt.reshape(batch * seq, d), mod_ctx, w,
                                      n_batch=batch, seq=seq, grid_rows=None, s0=None)
    new_state = jnp.stack([s_f, s_b], axis=1)[:, None].astype(x_prompt.dtype)

    y_sample, _, _ = _trunk_layer(x_sample.reshape(dec_batch * dec_seq, d), mod_dec, w,
                                  n_batch=dec_batch, seq=dec_seq, grid_rows=rows, s0=state_hgrn)
    return (y_prompt.reshape(batch, seq, d), y_sample.reshape(dec_batch, dec_seq, d), new_state)
```

```python
import functools

import jax
import jax.numpy as jnp
import numpy as np
from jax import lax
from jax.experimental import pallas as pl
from jax.experimental.pallas import tpu as pltpu

D_MODEL = 1024
D_FF = 2816
N_MOD = 9
EPS = 1e-6
GRID_W = 64
N_GROUPS = 4
GROUP = 128
FOUR_W = N_GROUPS * GROUP
N_HEADS = 4
HEAD = 128
HW = N_HEADS * HEAD
MIX_COLS = FOUR_W + 3 * HW + 2 * HW
GATE_COLS = 2 * D_MODEL

F32 = jnp.float32
BF16 = jnp.bfloat16

VMEM_LIMIT_BYTES = 56 * 1024 * 1024
FF_CHUNK = 256
TOKEN_TILE = 1024
SUB_TILE = 512
FFN_SUB_TILE_LEAD = 6
HGRN_CHUNK = 128
HGRN_SEQS = 8
HGRN_STAGE_LAG = 3
MOD_ROWS = 16
MOD_COL_TILE = 1536


def _params(*sem):
    return pltpu.CompilerParams(dimension_semantics=sem, vmem_limit_bytes=VMEM_LIMIT_BYTES)


def _const_spec(shape):
    nd = len(shape)
    return pl.BlockSpec(shape, lambda *_: (0,) * nd, pipeline_mode=pl.Buffered(1))


def _sigmoid(x):
    return 1.0 / (1.0 + jnp.exp(-x))


def _silu(x):
    return x * _sigmoid(x)


def _rms(x, gain):
    return x * lax.rsqrt(jnp.mean(x * x, axis=-1, keepdims=True) + EPS) * gain


def _dot(a, b):
    return jnp.dot(a, b, preferred_element_type=F32)


def _dot_nt(a, b):
    return lax.dot_general(a, b, (((1,), (1,)), ((), ())), preferred_element_type=F32)


def _mod_kernel(cond_ref, w_ref, b_ref, out_ref):
    a = _silu(cond_ref[...]).astype(BF16)
    out_ref[...] = _dot(a, w_ref[...].astype(BF16)) + b_ref[...]


def _modulation(cond, w_mod, b_mod):
    n = w_mod.shape[1]
    return pl.pallas_call(
        _mod_kernel,
        out_shape=jax.ShapeDtypeStruct((MOD_ROWS, n), F32),
        grid=(n // MOD_COL_TILE,),
        in_specs=[
            pl.BlockSpec((MOD_ROWS, D_MODEL), lambda j: (0, 0)),
            pl.BlockSpec((D_MODEL, MOD_COL_TILE), lambda j: (0, j)),
            pl.BlockSpec((1, MOD_COL_TILE), lambda j: (0, j)),
        ],
        out_specs=pl.BlockSpec((MOD_ROWS, MOD_COL_TILE), lambda j: (0, j)),
        compiler_params=_params("parallel"),
        name="modulation",
    )(cond, w_mod, b_mod.reshape(1, n))


def _sub_tiles():
    return [slice(s * SUB_TILE, (s + 1) * SUB_TILE) for s in range(TOKEN_TILE // SUB_TILE)]


def _skewed_order(n_streams, n_items, lead):
    order = []
    for pos in range(n_items + (n_streams - 1) * lead):
        for s in range(n_streams):
            j = pos - s * lead
            if 0 <= j < n_items:
                order.append((s, j))
    return order


def _ffn_kernel(x_ref, mod_ref, npre_ref, npost_ref, win_ref, wout_ref, out_ref, h_ref, acc_ref, *, mod_base):
    sh = mod_ref[0, mod_base:mod_base + 1, :]
    sc = mod_ref[0, mod_base + 1:mod_base + 2, :]
    gm = mod_ref[0, mod_base + 2:mod_base + 3, :]
    sub_tiles = _sub_tiles()
    n_chunks = D_FF // FF_CHUNK

    def prologue(s):
        h_ref[s] = (_rms(x_ref[sub_tiles[s], :], npre_ref[...]) * (1.0 + sc) + sh).astype(BF16)

    def epilogue(s):
        rows = sub_tiles[s]
        out_ref[rows, :] = x_ref[rows, :] + 0.5 * gm * _rms(acc_ref[s], npost_ref[...])

    def chunk(s, j):
        lo = j * FF_CHUNK
        h = h_ref[s]
        g = _dot(h, win_ref[:, lo:lo + FF_CHUNK])
        u = _dot(h, win_ref[:, D_FF + lo:D_FF + lo + FF_CHUNK])
        part = _dot((_silu(g) * u).astype(BF16), wout_ref[lo:lo + FF_CHUNK, :])
        if j == 0:
            acc_ref[s] = part
        else:
            acc_ref[s] += part

    order = _skewed_order(len(sub_tiles), n_chunks, FFN_SUB_TILE_LEAD)
    started = set()
    left = {s: n_chunks for s in range(len(sub_tiles))}
    for s, j in order:
        if s not in started:
            prologue(s)
            started.add(s)
        chunk(s, j)
        left[s] -= 1
        if left[s] == 0:
            epilogue(s)


def _ffn(x, mod, npre, npost, win, wout, *, mod_base, tiles_per_mod):
    t = x.shape[0]
    tm = TOKEN_TILE
    return pl.pallas_call(
        functools.partial(_ffn_kernel, mod_base=mod_base),
        out_shape=jax.ShapeDtypeStruct((t, D_MODEL), F32),
        grid=(t // tm,),
        in_specs=[
            pl.BlockSpec((tm, D_MODEL), lambda i: (i, 0)),
            pl.BlockSpec((1, N_MOD, D_MODEL), lambda i: (i // tiles_per_mod, 0, 0)),
            _const_spec((1, D_MODEL)),
            _const_spec((1, D_MODEL)),
            _const_spec((D_MODEL, 2 * D_FF)),
            _const_spec((D_FF, D_MODEL)),
        ],
        out_specs=pl.BlockSpec((tm, D_MODEL), lambda i: (i, 0)),
        scratch_shapes=[pltpu.VMEM((tm // SUB_TILE, SUB_TILE, D_MODEL), BF16),
                        pltpu.VMEM((tm // SUB_TILE, SUB_TILE, D_MODEL), F32)],
        compiler_params=_params("parallel"),
        name="ffn_half_step",
    )(x, mod, npre, npost, win, wout)


SUBLANES = 8
CHUNK_VREGS = HGRN_CHUNK // SUBLANES


def _chunk_token_of_row(row):
    return (row & (SUBLANES - 1)) * CHUNK_VREGS + (row >> 3)


def _chunk_perm_matrix(transpose):
    c = HGRN_CHUNK
    a = lax.broadcasted_iota(jnp.int32, (c, c), 0)
    b = lax.broadcasted_iota(jnp.int32, (c, c), 1)
    hit = (a == _chunk_token_of_row(b)) if transpose else (b == _chunk_token_of_row(a))
    return jnp.where(hit, 1.0, 0.0).astype(BF16)


def _mixer_input(x, mod_ref, npre_ref):
    sh = mod_ref[0, 3:4, :]
    sc = mod_ref[0, 4:5, :]
    return (_rms(x, npre_ref[...]) * (1.0 + sc) + sh).astype(BF16)


def _proj_kernel(x_ref, mod_ref, npre_ref, lbl_ref, win_ref,
                 u_ref, q_ref, ff_ref, fb_ref, v_ref, g_ref):
    l0 = lbl_ref[0]
    l1 = lbl_ref[1]
    mx = jnp.maximum(l0, l1)
    e0 = jnp.exp(l0 - mx)
    e1 = jnp.exp(l1 - mx)
    lb = e0 / (e0 + e1)
    perm = _chunk_perm_matrix(transpose=False)
    c = HGRN_CHUNK

    def cols(lhs, lo, n):
        return _dot(lhs, win_ref[:, lo:lo + n])

    for rows in _sub_tiles():
        h = _mixer_input(x_ref[rows, :], mod_ref, npre_ref)
        hp = jnp.concatenate([_dot(perm, h[j * c:(j + 1) * c]) for j in range(SUB_TILE // c)],
                             axis=0).astype(BF16)
        u_ref[rows, :] = cols(h, 0, FOUR_W).astype(BF16)
        q_ref[rows, :] = _silu(cols(hp, FOUR_W, HW)).astype(BF16)
        ff_ref[rows, :] = lb[0:1, :] + (1.0 - lb[0:1, :]) * _sigmoid(cols(hp, FOUR_W + HW, HW))
        fb_ref[rows, :] = lb[1:2, :] + (1.0 - lb[1:2, :]) * _sigmoid(cols(hp, FOUR_W + 2 * HW, HW))
        v_ref[rows, :] = cols(hp, FOUR_W + 3 * HW, HW).astype(BF16)
        g_ref[rows, :] = _silu(cols(hp, FOUR_W + 4 * HW, HW)).astype(BF16)


def _proj(x, mod, npre, lb_logits, win, *, tiles_per_mod):
    t = x.shape[0]
    tm = TOKEN_TILE

    def tok(n):
        return pl.BlockSpec((tm, n), lambda i: (i, 0))

    shapes = [
        jax.ShapeDtypeStruct((t, FOUR_W), BF16),
        jax.ShapeDtypeStruct((t, HW), BF16),
        jax.ShapeDtypeStruct((t, HW), F32),
        jax.ShapeDtypeStruct((t, HW), F32),
        jax.ShapeDtypeStruct((t, HW), BF16),
        jax.ShapeDtypeStruct((t, HW), BF16),
    ]
    return pl.pallas_call(
        _proj_kernel,
        out_shape=shapes,
        grid=(t // tm,),
        in_specs=[
            tok(D_MODEL),
            pl.BlockSpec((1, N_MOD, D_MODEL), lambda i: (i // tiles_per_mod, 0, 0)),
            _const_spec((1, D_MODEL)),
            _const_spec((2, 2, HW)),
            _const_spec((D_MODEL, MIX_COLS)),
        ],
        out_specs=[tok(FOUR_W), tok(HW), tok(HW), tok(HW), tok(HW), tok(HW)],
        compiler_params=_params("parallel"),
        name="mixer_in_proj",
    )(x, mod, npre, lb_logits, win)


def _cos_sin(n):
    k = np.arange(n)
    ang = 2.0 * np.pi * ((k[:, None] * k[None, :]) % n) / n
    return np.cos(ang), np.sin(ang)


def _bf16_const(a):
    return jnp.asarray(a, F32).astype(BF16)


def _channel_dft(u):
    def go(cs_ref):
        a_parts, b_parts = [], []
        for g in range(N_GROUPS):
            ab = _dot(u[:, g * GROUP:(g + 1) * GROUP], cs_ref[...])
            a_parts.append(ab[:, :GROUP])
            b_parts.append(ab[:, GROUP:])
        return (jnp.concatenate(a_parts, axis=1).astype(BF16),
                jnp.concatenate(b_parts, axis=1).astype(BF16))
    return go


def _fourier_seq_kernel(u_ref, cs_ref, pos_ref, y_ref, *, seq, n_seq):
    for s in range(n_seq):
        a, b = _channel_dft(u_ref[s * seq:(s + 1) * seq, :])(cs_ref)
        ab = jnp.concatenate([a, b], axis=0)
        y_ref[s * seq:(s + 1) * seq, :] = _dot(pos_ref[...], ab).astype(BF16)


def _fourier_seq(u, seq):
    t = u.shape[0]
    n_seq = 4
    cc, sc = _cos_sin(GROUP)
    cl, sl = _cos_sin(seq)
    cs = _bf16_const(np.concatenate([cc, sc], axis=1) / np.sqrt(GROUP))
    pos = _bf16_const(np.concatenate([cl, -sl], axis=1) / np.sqrt(seq))
    rows = n_seq * seq
    return pl.pallas_call(
        functools.partial(_fourier_seq_kernel, seq=seq, n_seq=n_seq),
        out_shape=jax.ShapeDtypeStruct((t, FOUR_W), BF16),
        grid=(t // rows,),
        in_specs=[
            pl.BlockSpec((rows, FOUR_W), lambda i: (i, 0)),
            _const_spec((GROUP, 2 * GROUP)),
            _const_spec((seq, 2 * seq)),
        ],
        out_specs=pl.BlockSpec((rows, FOUR_W), lambda i: (i, 0)),
        compiler_params=_params("parallel"),
        name="fourier_seq",
    )(u, cs, pos)


W_BLOCK = 8
CHANNEL_DFT_ROWS = 512


def _fourier_grid_kernel(u_ref, cs_ref, m1_ref, k2_ref, y_ref, ab_ref, p_ref, q_ref, *, n_rows):
    seq = n_rows * GRID_W
    for blk in range(seq // CHANNEL_DFT_ROWS):
        rows = slice(blk * CHANNEL_DFT_ROWS, (blk + 1) * CHANNEL_DFT_ROWS)
        a, b = _channel_dft(u_ref[rows, :])(cs_ref)
        ab_ref[0, rows, :] = a
        ab_ref[1, rows, :] = b

    def row_body(r, carry):
        start = pl.multiple_of(r * GRID_W, GRID_W)
        ab = jnp.concatenate([ab_ref[0, pl.ds(start, GRID_W), :], ab_ref[1, pl.ds(start, GRID_W), :]], axis=0)
        pq = _dot(m1_ref[...], ab)
        p_ref[r] = pq[:GRID_W]
        q_ref[r] = pq[GRID_W:]
        return carry

    lax.fori_loop(0, n_rows, row_body, 0, unroll=2)

    for j in range(GRID_W // W_BLOCK):
        ws = slice(j * W_BLOCK, (j + 1) * W_BLOCK)
        pb = p_ref[:, ws, :].reshape(n_rows * W_BLOCK, FOUR_W).astype(BF16)
        qb = q_ref[:, ws, :].reshape(n_rows * W_BLOCK, FOUR_W).astype(BF16)
        yb = _dot(k2_ref[...], jnp.concatenate([pb, qb], axis=0))
        p_ref[:, ws, :] = yb.reshape(n_rows, W_BLOCK, FOUR_W)
    for r0 in range(0, n_rows, SUBLANES):
        y_ref[0, r0:r0 + SUBLANES] = p_ref[r0:r0 + SUBLANES].astype(BF16)


def _fourier_grid(u, n_batch, n_rows):
    cc, sc = _cos_sin(GROUP)
    cw, sw = _cos_sin(GRID_W)
    cr, sr = _cos_sin(n_rows)
    cs = _bf16_const(np.concatenate([cc, sc], axis=1) / np.sqrt(GROUP))
    m1 = _bf16_const(np.block([[cw, -sw], [sw, cw]]) / np.sqrt(GRID_W))
    eye = np.eye(W_BLOCK)
    k2 = _bf16_const(np.concatenate([np.kron(cr, eye), -np.kron(sr, eye)], axis=1) / np.sqrt(n_rows))
    seq = n_rows * GRID_W
    nk = n_rows * W_BLOCK
    y = pl.pallas_call(
        functools.partial(_fourier_grid_kernel, n_rows=n_rows),
        out_shape=jax.ShapeDtypeStruct((n_batch, n_rows, GRID_W, FOUR_W), BF16),
        grid=(n_batch,),
        in_specs=[
            pl.BlockSpec((seq, FOUR_W), lambda b: (b, 0)),
            _const_spec((GROUP, 2 * GROUP)),
            _const_spec((2 * GRID_W, 2 * GRID_W)),
            _const_spec((nk, 2 * nk)),
        ],
        out_specs=pl.BlockSpec((1, n_rows, GRID_W, FOUR_W), lambda b: (b, 0, 0, 0)),
        scratch_shapes=[pltpu.VMEM((2, seq, FOUR_W), BF16),
                        pltpu.VMEM((n_rows, GRID_W, FOUR_W), F32),
                        pltpu.VMEM((n_rows, GRID_W, FOUR_W), F32)],
        compiler_params=_params("parallel"),
        name="fourier_grid",
    )(u, cs, m1, k2)
    return y.reshape(n_batch * seq, FOUR_W)


def _hgrn_kernel(*refs, reverse, zero_init, final, n_seq):
    it = iter(refs)
    q_ref, f_ref, v_ref = next(it), next(it), next(it)
    s0_ref = None if zero_init else next(it)
    if final:
        oprev_ref, gs_ref, gain_ref = next(it), next(it), next(it)
    o_ref, sfin_ref, st_ref = next(it), next(it), next(it)

    i = pl.program_id(1)
    c = HGRN_CHUNK
    nv = CHUNK_VREGS

    @pl.when(i == 0)
    def _():
        for sq in range(n_seq):
            for hd in range(N_HEADS):
                if zero_init:
                    st_ref[sq * N_HEADS + hd] = jnp.zeros((HEAD, HEAD), F32)
                else:
                    st_ref[sq * N_HEADS + hd] = s0_ref[sq, 0, 0, hd].T

    row_a = lax.broadcasted_iota(jnp.int32, (c, c), 0)
    row_b = lax.broadcasted_iota(jnp.int32, (c, c), 1)
    tt = _chunk_token_of_row(row_a)
    ss = _chunk_token_of_row(row_b)
    xor = tt ^ ss
    ordered = (tt < ss) if reverse else (tt > ss)
    vreg_levels = [1 << b for b in range(nv.bit_length() - 1)]
    sub_levels = [1 << b for b in range(SUBLANES.bit_length() - 1)]
    level_mask = {h: ordered & (xor >= h) & (xor < 2 * h) for h in vreg_levels + [nv * g for g in sub_levels]}
    diag_mask = row_a == row_b
    sub = lax.broadcasted_iota(jnp.int32, (SUBLANES, HEAD), 0)
    out_bit = 0 if reverse else 1
    if final:
        unperm = _chunk_perm_matrix(transpose=True)

    def vregs(x):
        return [x[SUBLANES * j:SUBLANES * (j + 1)] for j in range(nv)]

    def operand(xs):
        return jnp.concatenate(xs, axis=0).astype(BF16)

    def times(x, m):
        return m if x is None else x * m

    def lanes(hd):
        return slice(hd * HEAD, (hd + 1) * HEAD)

    def level_stage(sq, hd):
        sl = lanes(hd)
        f = f_ref[sq, :, sl]
        q = q_ref[sq, :, sl].astype(F32)
        vb = v_ref[sq, :, sl]
        k = 1.0 - f
        qv, kv = vregs(q), vregs(k)
        p = vregs(f)
        r = [None] * nv
        tot = vregs(f)
        a = jnp.zeros((c, c), F32)
        for h in vreg_levels:
            is_out = [((x // h) & 1) == out_bit for x in range(nv)]
            qop = operand([qv[x] * p[x] if is_out[x] else qv[x] for x in range(nv)])
            kop = operand([kv[x] if is_out[x] else times(r[x], kv[x]) for x in range(nv)])
            a = jnp.where(level_mask[h], _dot_nt(qop, kop), a)
            new_tot = []
            for b2 in range(len(tot) // 2):
                lo, hi = tot[2 * b2], tot[2 * b2 + 1]
                lo_rows = range(2 * b2 * h, (2 * b2 + 1) * h)
                hi_rows = range((2 * b2 + 1) * h, (2 * b2 + 2) * h)
                if reverse:
                    for x in lo_rows:
                        p[x] = p[x] * hi
                    for x in hi_rows:
                        r[x] = times(r[x], lo)
                else:
                    for x in hi_rows:
                        p[x] = p[x] * lo
                    for x in lo_rows:
                        r[x] = times(r[x], hi)
                new_tot.append(lo * hi)
            tot = new_tot
        t8 = tot[0]
        for g in sub_levels:
            qop = operand([qv[x] * p[x] for x in range(nv)])
            kop = operand([times(r[x], kv[x]) for x in range(nv)])
            a = jnp.where(level_mask[nv * g], _dot_nt(qop, kop), a)
            hi_half = (sub & g) != 0
            w = jnp.where(hi_half, pltpu.roll(t8, g, 0), pltpu.roll(t8, SUBLANES - g, 0))
            grow_p = hi_half != reverse
            mp = jnp.where(grow_p, w, 1.0)
            mr = jnp.where(grow_p, 1.0, w)
            p = [x * mp for x in p]
            r = [times(x, mr) for x in r]
            t8 = t8 * w
        a = jnp.where(diag_mask, jnp.sum(q * k, axis=-1, keepdims=True), a)
        qdec = operand([qv[x] * p[x] for x in range(nv)])
        kdec = operand([kv[x] * r[x] for x in range(nv)])
        return a.astype(BF16), qdec, kdec, t8[0:1, :], vb

    def state_stage(sq, hd, a, qdec, kdec, total, vb):
        slot = sq * N_HEADS + hd
        st = st_ref[slot]
        o = _dot(a, vb) + _dot_nt(qdec, st.astype(BF16))
        vt = vb.astype(F32).T.astype(BF16)
        st_ref[slot] = st * total + _dot(vt, kdec)
        return o

    def output_stage(sq, hd, o):
        sl = lanes(hd)
        if final:
            o = o + oprev_ref[sq, :, sl]
            o = o * lax.rsqrt(jnp.mean(o * o, axis=-1, keepdims=True) + EPS)
            o = o * gain_ref[:, sl] * gs_ref[sq, :, sl].astype(F32)
            o_ref[sq, :, sl] = _dot(unperm, o.astype(BF16)).astype(BF16)
        else:
            o_ref[sq, :, sl] = o

    units = [(sq, hd) for sq in range(n_seq) for hd in range(N_HEADS)]
    scores, outs = {}, {}
    for step in range(len(units) + 2 * HGRN_STAGE_LAG):
        if step < len(units):
            scores[step] = level_stage(*units[step])
        j = step - HGRN_STAGE_LAG
        if 0 <= j < len(units):
            outs[j] = state_stage(*units[j], *scores.pop(j))
        j = step - 2 * HGRN_STAGE_LAG
        if 0 <= j < len(units):
            output_stage(*units[j], outs.pop(j))

    @pl.when(i == pl.num_programs(1) - 1)
    def _():
        for sq in range(n_seq):
            for hd in range(N_HEADS):
                sfin_ref[sq, hd] = st_ref[sq * N_HEADS + hd].T


def _hgrn(q, f, v, *, n_batch, seq, reverse, s0=None, s0_dir=0, final_inputs=None):
    c = HGRN_CHUNK
    ns = HGRN_SEQS
    n_chunks = seq // c
    final = final_inputs is not None

    def tok_map(b, i):
        return (b, (n_chunks - 1 - i) if reverse else i, 0)

    tok = pl.BlockSpec((ns, c, HW), tok_map)
    in_specs = [tok, tok, tok]
    args = [q, f, v]
    if s0 is not None:
        in_specs.append(pl.BlockSpec((ns, 1, 1, N_HEADS, HEAD, HEAD), lambda b, i: (b, 0, s0_dir, 0, 0, 0)))
        args.append(s0)
    if final:
        oprev, gs, gain = final_inputs
        in_specs += [tok, tok, pl.BlockSpec((1, HW), lambda b, i: (0, 0))]
        args += [oprev, gs, gain]
    out_shape = [jax.ShapeDtypeStruct((n_batch, seq, HW), BF16 if final else F32),
                 jax.ShapeDtypeStruct((n_batch, N_HEADS, HEAD, HEAD), F32)]
    out_specs = [tok, pl.BlockSpec((ns, N_HEADS, HEAD, HEAD), lambda b, i: (b, 0, 0, 0))]
    return pl.pallas_call(
        functools.partial(_hgrn_kernel, reverse=reverse, zero_init=s0 is None, final=final, n_seq=ns),
        out_shape=out_shape,
        grid=(n_batch // ns, n_chunks),
        in_specs=in_specs,
        out_specs=out_specs,
        scratch_shapes=[pltpu.VMEM((ns * N_HEADS, HEAD, HEAD), F32)],
        compiler_params=_params("arbitrary", "arbitrary"),
        name="hgrn2_bwd" if reverse else "hgrn2_fwd",
    )(*args)


def _merge_kernel(x_ref, y_ref, o_ref, mod_ref, npre_ref, npost_ref, wg_ref, wf_ref, wh_ref, wo_ref, out_ref):
    for rows in _sub_tiles():
        x = x_ref[rows, :]
        h = _mixer_input(x, mod_ref, npre_ref)
        merged = (_sigmoid(_dot(h, wg_ref[:, :D_MODEL])) * _dot(y_ref[rows, :], wf_ref[...])
                  + _sigmoid(_dot(h, wg_ref[:, D_MODEL:])) * _dot(o_ref[rows, :], wh_ref[...]))
        m = _dot(merged.astype(BF16), wo_ref[...])
        out_ref[rows, :] = x + mod_ref[0, 5:6, :] * _rms(m, npost_ref[...])


def _merge(x, y, o, mod, npre, npost, wg, wf, wh, wo, *, tiles_per_mod):
    t = x.shape[0]
    tm = TOKEN_TILE

    def tok(n):
        return pl.BlockSpec((tm, n), lambda i: (i, 0))

    return pl.pallas_call(
        _merge_kernel,
        out_shape=jax.ShapeDtypeStruct((t, D_MODEL), F32),
        grid=(t // tm,),
        in_specs=[
            tok(D_MODEL), tok(FOUR_W), tok(HW),
            pl.BlockSpec((1, N_MOD, D_MODEL), lambda i: (i // tiles_per_mod, 0, 0)),
            _const_spec((1, D_MODEL)),
            _const_spec((1, D_MODEL)),
            _const_spec((D_MODEL, GATE_COLS)),
            _const_spec((FOUR_W, D_MODEL)),
            _const_spec((HW, D_MODEL)),
            _const_spec((D_MODEL, D_MODEL)),
        ],
        out_specs=tok(D_MODEL),
        compiler_params=_params("parallel"),
        name="mixer_merge",
    )(x, y, o, mod, npre, npost, wg, wf, wh, wo)


def _trunk_layer(x, mod, w, *, n_batch, seq, grid_rows, s0):
    n_tiles = x.shape[0] // TOKEN_TILE
    tiles_per_mod = n_tiles // mod.shape[0]
    x1 = _ffn(x, mod, w["npre"][0], w["npost"][0], w["ffn_in"][0], w["ffn_out"][0],
              mod_base=0, tiles_per_mod=tiles_per_mod)
    u, q, ff, fb, v, gs = _proj(x1, mod, w["npre"][1], w["lb_logits"], w["w_mix"], tiles_per_mod=tiles_per_mod)
    if grid_rows is None:
        y = _fourier_seq(u, seq)
    else:
        y = _fourier_grid(u, n_batch, grid_rows)
    q, ff, fb, v, gs = (z.reshape(n_batch, seq, HW) for z in (q, ff, fb, v, gs))
    o_f, s_f = _hgrn(q, ff, v, n_batch=n_batch, seq=seq, reverse=False, s0=s0, s0_dir=0)
    o, s_b = _hgrn(q, fb, v, n_batch=n_batch, seq=seq, reverse=True, s0=s0, s0_dir=1,
                   final_inputs=(o_f, gs, w["hgrn_gain"]))
    x2 = _merge(x1, y, o.reshape(n_batch * seq, HW), mod, w["npre"][1], w["npost"][1], w["w_gate"],
                w["w_four"], w["w_hgrn"], w["w_out"], tiles_per_mod=tiles_per_mod)
    out = _ffn(x2, mod, w["npre"][2], w["npost"][2], w["ffn_in"][1], w["ffn_out"][1],
               mod_base=6, tiles_per_mod=tiles_per_mod)
    return out, s_f, s_b


def kernel(x_prompt, x_sample, state_hgrn, c, c_ctx, w_mod, b_mod, norm_pre, norm_post, ffn_w_in, ffn_w_out,
           w_in, w_four, hgrn_gain, w_hgrn, w_out, lb_logits):
    batch, seq, d = x_prompt.shape
    dec_batch, dec_seq, _ = x_sample.shape
    depth = w_mod.shape[0]
    assert depth == 1 and d == D_MODEL
    rows = dec_seq // GRID_W
    layer = 0

    cond = jnp.concatenate([c_ctx[None, :], c, jnp.zeros((MOD_ROWS - 1 - dec_batch, d), F32)], axis=0)
    mod = _modulation(cond, w_mod[layer], b_mod[layer]).reshape(MOD_ROWS, N_MOD, d)
    mod_ctx = mod[0:1]
    mod_dec = mod[1:1 + dec_batch]

    w = dict(
        npre=norm_pre[layer].reshape(3, 1, d),
        npost=norm_post[layer].reshape(3, 1, d),
        ffn_in=[ffn_w_in[layer, k].astype(BF16) for k in range(2)],
        ffn_out=[ffn_w_out[layer, k].astype(BF16) for k in range(2)],
        w_mix=w_in[layer, :, :MIX_COLS].astype(BF16),
        w_gate=w_in[layer, :, MIX_COLS:].astype(BF16),
        w_four=w_four[layer].astype(BF16),
        w_hgrn=w_hgrn[layer].astype(BF16),
        w_out=w_out[layer].astype(BF16),
        hgrn_gain=hgrn_gain[layer].reshape(1, HW),
        lb_logits=lb_logits[layer:layer + 2],
    )

    y_prompt, s_f, s_b = _trunk_layer(x_prompt.reshape(batch * seq, d), mod_ctx, w,
                                      n_batch=batch, seq=seq, grid_rows=None, s0=None)
    new_state = jnp.stack([s_f, s_b], axis=1)[:, None].astype(x_prompt.dtype)

    y_sample, _, _ = _trunk_layer(x_sample.reshape(dec_batch * dec_seq, d), mod_dec, w,
                                  n_batch=dec_batch, seq=dec_seq, grid_rows=rows, s0=state_hgrn)
    return (y_prompt.reshape(batch, seq, d), y_sample.reshape(dec_batch, dec_seq, d), new_state)
```

```python
import functools

import jax
import jax.numpy as jnp
import numpy as np
from jax import lax
from jax.experimental import pallas as pl
from jax.experimental.pallas import tpu as pltpu

D_MODEL = 1024
D_FF = 2816
N_MOD = 9
EPS = 1e-6
GRID_W = 64
N_GROUPS = 4
GROUP = 128
FOUR_W = N_GROUPS * GROUP
N_HEADS = 4
HEAD = 128
HW = N_HEADS * HEAD
MIX_COLS = FOUR_W + 3 * HW + 2 * HW
GATE_COLS = 2 * D_MODEL

F32 = jnp.float32
BF16 = jnp.bfloat16

VMEM_LIMIT_BYTES = 56 * 1024 * 1024
FF_CHUNK = 256
TOKEN_TILE = 1024
SUB_TILE = 512
FFN_SUB_TILE_LEAD = 6
HGRN_CHUNK = 128
HGRN_SEQS = 8
HGRN_STAGE_LAG = 3
MOD_ROWS = 16
MOD_COL_TILE = 1536


def _params(*sem):
    return pltpu.CompilerParams(dimension_semantics=sem, vmem_limit_bytes=VMEM_LIMIT_BYTES)


def _const_spec(shape):
    nd = len(shape)
    return pl.BlockSpec(shape, lambda *_: (0,) * nd, pipeline_mode=pl.Buffered(1))


def _slab_spec(block_shape, index):
    return pl.BlockSpec(block_shape, lambda *_: index, pipeline_mode=pl.Buffered(1))


def _sigmoid(x):
    return 1.0 / (1.0 + jnp.exp(-x))


def _silu(x):
    return x * _sigmoid(x)


def _rms(x, gain):
    return x * lax.rsqrt(jnp.mean(x * x, axis=-1, keepdims=True) + EPS) * gain


def _dot(a, b):
    return jnp.dot(a, b, preferred_element_type=F32)


def _dot_nt(a, b):
    return lax.dot_general(a, b, (((1,), (1,)), ((), ())), preferred_element_type=F32)


def _mod_kernel(cond_ref, w_ref, b_ref, out_ref):
    a = _silu(cond_ref[...]).astype(BF16)
    out_ref[...] = _dot(a, w_ref[...].astype(BF16)) + b_ref[...]


def _modulation(cond, w_mod, b_mod):
    n = w_mod.shape[1]
    return pl.pallas_call(
        _mod_kernel,
        out_shape=jax.ShapeDtypeStruct((MOD_ROWS, n), F32),
        grid=(n // MOD_COL_TILE,),
        in_specs=[
            pl.BlockSpec((MOD_ROWS, D_MODEL), lambda j: (0, 0)),
            pl.BlockSpec((D_MODEL, MOD_COL_TILE), lambda j: (0, j)),
            pl.BlockSpec((1, MOD_COL_TILE), lambda j: (0, j)),
        ],
        out_specs=pl.BlockSpec((MOD_ROWS, MOD_COL_TILE), lambda j: (0, j)),
        compiler_params=_params("parallel"),
        name="modulation",
    )(cond, w_mod, b_mod.reshape(1, n))


def _sub_tiles():
    return [slice(s * SUB_TILE, (s + 1) * SUB_TILE) for s in range(TOKEN_TILE // SUB_TILE)]


def _skewed_order(n_streams, n_items, lead):
    order = []
    for pos in range(n_items + (n_streams - 1) * lead):
        for s in range(n_streams):
            j = pos - s * lead
            if 0 <= j < n_items:
                order.append((s, j))
    return order


def _ffn_kernel(x_ref, mod_ref, npre_ref, npost_ref, win_ref, wout_ref, out_ref, h_ref, acc_ref, *, mod_base):
    sh = mod_ref[0, mod_base:mod_base + 1, :]
    sc = mod_ref[0, mod_base + 1:mod_base + 2, :]
    gm = mod_ref[0, mod_base + 2:mod_base + 3, :]
    sub_tiles = _sub_tiles()
    n_chunks = D_FF // FF_CHUNK

    def prologue(s):
        h_ref[s] = (_rms(x_ref[sub_tiles[s], :], npre_ref[...]) * (1.0 + sc) + sh).astype(BF16)

    def epilogue(s):
        rows = sub_tiles[s]
        out_ref[rows, :] = x_ref[rows, :] + 0.5 * gm * _rms(acc_ref[s], npost_ref[...])

    def chunk(s, j):
        lo = j * FF_CHUNK
        h = h_ref[s]
        g = _dot(h, win_ref[:, lo:lo + FF_CHUNK])
        u = _dot(h, win_ref[:, D_FF + lo:D_FF + lo + FF_CHUNK])
        part = _dot((_silu(g) * u).astype(BF16), wout_ref[lo:lo + FF_CHUNK, :])
        if j == 0:
            acc_ref[s] = part
        else:
            acc_ref[s] += part

    order = _skewed_order(len(sub_tiles), n_chunks, FFN_SUB_TILE_LEAD)
    started = set()
    left = {s: n_chunks for s in range(len(sub_tiles))}
    for s, j in order:
        if s not in started:
            prologue(s)
            started.add(s)
        chunk(s, j)
        left[s] -= 1
        if left[s] == 0:
            epilogue(s)


def _ffn(x, mod, npre, npost, win, wout, *, layer, ffn_index, mod_base, tiles_per_mod):
    t = x.shape[0]
    tm = TOKEN_TILE
    return pl.pallas_call(
        functools.partial(_ffn_kernel, mod_base=mod_base),
        out_shape=jax.ShapeDtypeStruct((t, D_MODEL), F32),
        grid=(t // tm,),
        in_specs=[
            pl.BlockSpec((tm, D_MODEL), lambda i: (i, 0)),
            pl.BlockSpec((1, N_MOD, D_MODEL), lambda i: (i // tiles_per_mod, 0, 0)),
            _const_spec((1, D_MODEL)),
            _const_spec((1, D_MODEL)),
            _slab_spec((None, None, D_MODEL, 2 * D_FF), (layer, ffn_index, 0, 0)),
            _slab_spec((None, None, D_FF, D_MODEL), (layer, ffn_index, 0, 0)),
        ],
        out_specs=pl.BlockSpec((tm, D_MODEL), lambda i: (i, 0)),
        scratch_shapes=[pltpu.VMEM((tm // SUB_TILE, SUB_TILE, D_MODEL), BF16),
                        pltpu.VMEM((tm // SUB_TILE, SUB_TILE, D_MODEL), F32)],
        compiler_params=_params("parallel"),
        name="ffn_half_step",
    )(x, mod, npre, npost, win, wout)


SUBLANES = 8
CHUNK_VREGS = HGRN_CHUNK // SUBLANES


def _chunk_token_of_row(row):
    return (row & (SUBLANES - 1)) * CHUNK_VREGS + (row >> 3)


def _chunk_perm_matrix(transpose):
    c = HGRN_CHUNK
    a = lax.broadcasted_iota(jnp.int32, (c, c), 0)
    b = lax.broadcasted_iota(jnp.int32, (c, c), 1)
    hit = (a == _chunk_token_of_row(b)) if transpose else (b == _chunk_token_of_row(a))
    return jnp.where(hit, 1.0, 0.0).astype(BF16)


def _mixer_input(x, mod_ref, npre_ref):
    sh = mod_ref[0, 3:4, :]
    sc = mod_ref[0, 4:5, :]
    return (_rms(x, npre_ref[...]) * (1.0 + sc) + sh).astype(BF16)


def _proj_kernel(x_ref, mod_ref, npre_ref, lbl_ref, win_ref,
                 u_ref, q_ref, ff_ref, fb_ref, v_ref, g_ref):
    l0 = lbl_ref[0]
    l1 = lbl_ref[1]
    mx = jnp.maximum(l0, l1)
    e0 = jnp.exp(l0 - mx)
    e1 = jnp.exp(l1 - mx)
    lb = e0 / (e0 + e1)
    perm = _chunk_perm_matrix(transpose=False)
    c = HGRN_CHUNK

    def cols(lhs, lo, n):
        return _dot(lhs, win_ref[:, lo:lo + n])

    for rows in _sub_tiles():
        h = _mixer_input(x_ref[rows, :], mod_ref, npre_ref)
        hp = jnp.concatenate([_dot(perm, h[j * c:(j + 1) * c]) for j in range(SUB_TILE // c)],
                             axis=0).astype(BF16)
        u_ref[rows, :] = cols(h, 0, FOUR_W).astype(BF16)
        q_ref[rows, :] = _silu(cols(hp, FOUR_W, HW)).astype(BF16)
        ff_ref[rows, :] = lb[0:1, :] + (1.0 - lb[0:1, :]) * _sigmoid(cols(hp, FOUR_W + HW, HW))
        fb_ref[rows, :] = lb[1:2, :] + (1.0 - lb[1:2, :]) * _sigmoid(cols(hp, FOUR_W + 2 * HW, HW))
        v_ref[rows, :] = cols(hp, FOUR_W + 3 * HW, HW).astype(BF16)
        g_ref[rows, :] = _silu(cols(hp, FOUR_W + 4 * HW, HW)).astype(BF16)


def _proj(x, mod, npre, lb_logits, win, *, layer, tiles_per_mod):
    t = x.shape[0]
    tm = TOKEN_TILE

    def tok(n):
        return pl.BlockSpec((tm, n), lambda i: (i, 0))

    shapes = [
        jax.ShapeDtypeStruct((t, FOUR_W), BF16),
        jax.ShapeDtypeStruct((t, HW), BF16),
        jax.ShapeDtypeStruct((t, HW), F32),
        jax.ShapeDtypeStruct((t, HW), F32),
        jax.ShapeDtypeStruct((t, HW), BF16),
        jax.ShapeDtypeStruct((t, HW), BF16),
    ]
    return pl.pallas_call(
        _proj_kernel,
        out_shape=shapes,
        grid=(t // tm,),
        in_specs=[
            tok(D_MODEL),
            pl.BlockSpec((1, N_MOD, D_MODEL), lambda i: (i // tiles_per_mod, 0, 0)),
            _const_spec((1, D_MODEL)),
            _const_spec((2, 2, HW)),
            _slab_spec((None, D_MODEL, MIX_COLS), (layer, 0, 0)),
        ],
        out_specs=[tok(FOUR_W), tok(HW), tok(HW), tok(HW), tok(HW), tok(HW)],
        compiler_params=_params("parallel"),
        name="mixer_in_proj",
    )(x, mod, npre, lb_logits, win)


def _cos_sin(n):
    k = np.arange(n)
    ang = 2.0 * np.pi * ((k[:, None] * k[None, :]) % n) / n
    return np.cos(ang), np.sin(ang)


def _bf16_const(a):
    return jnp.asarray(a, F32).astype(BF16)


def _channel_dft(u):
    def go(cs_ref):
        a_parts, b_parts = [], []
        for g in range(N_GROUPS):
            ab = _dot(u[:, g * GROUP:(g + 1) * GROUP], cs_ref[...])
            a_parts.append(ab[:, :GROUP])
            b_parts.append(ab[:, GROUP:])
        return (jnp.concatenate(a_parts, axis=1).astype(BF16),
                jnp.concatenate(b_parts, axis=1).astype(BF16))
    return go


def _fourier_seq_kernel(u_ref, cs_ref, pos_ref, y_ref, *, seq, n_seq):
    for s in range(n_seq):
        a, b = _channel_dft(u_ref[s * seq:(s + 1) * seq, :])(cs_ref)
        ab = jnp.concatenate([a, b], axis=0)
        y_ref[s * seq:(s + 1) * seq, :] = _dot(pos_ref[...], ab).astype(BF16)


def _fourier_seq(u, seq):
    t = u.shape[0]
    n_seq = 4
    cc, sc = _cos_sin(GROUP)
    cl, sl = _cos_sin(seq)
    cs = _bf16_const(np.concatenate([cc, sc], axis=1) / np.sqrt(GROUP))
    pos = _bf16_const(np.concatenate([cl, -sl], axis=1) / np.sqrt(seq))
    rows = n_seq * seq
    return pl.pallas_call(
        functools.partial(_fourier_seq_kernel, seq=seq, n_seq=n_seq),
        out_shape=jax.ShapeDtypeStruct((t, FOUR_W), BF16),
        grid=(t // rows,),
        in_specs=[
            pl.BlockSpec((rows, FOUR_W), lambda i: (i, 0)),
            _const_spec((GROUP, 2 * GROUP)),
            _const_spec((seq, 2 * seq)),
        ],
        out_specs=pl.BlockSpec((rows, FOUR_W), lambda i: (i, 0)),
        compiler_params=_params("parallel"),
        name="fourier_seq",
    )(u, cs, pos)


W_BLOCK = 8
CHANNEL_DFT_ROWS = 512


def _fourier_grid_kernel(u_ref, cs_ref, m1_ref, k2_ref, y_ref, ab_ref, p_ref, q_ref, *, n_rows):
    seq = n_rows * GRID_W
    for blk in range(seq // CHANNEL_DFT_ROWS):
        rows = slice(blk * CHANNEL_DFT_ROWS, (blk + 1) * CHANNEL_DFT_ROWS)
        a, b = _channel_dft(u_ref[rows, :])(cs_ref)
        ab_ref[0, rows, :] = a
        ab_ref[1, rows, :] = b

    def row_body(r, carry):
        start = pl.multiple_of(r * GRID_W, GRID_W)
        ab = jnp.concatenate([ab_ref[0, pl.ds(start, GRID_W), :], ab_ref[1, pl.ds(start, GRID_W), :]], axis=0)
        pq = _dot(m1_ref[...], ab)
        p_ref[r] = pq[:GRID_W]
        q_ref[r] = pq[GRID_W:]
        return carry

    lax.fori_loop(0, n_rows, row_body, 0, unroll=2)

    for j in range(GRID_W // W_BLOCK):
        ws = slice(j * W_BLOCK, (j + 1) * W_BLOCK)
        pb = p_ref[:, ws, :].reshape(n_rows * W_BLOCK, FOUR_W).astype(BF16)
        qb = q_ref[:, ws, :].reshape(n_rows * W_BLOCK, FOUR_W).astype(BF16)
        yb = _dot(k2_ref[...], jnp.concatenate([pb, qb], axis=0))
        p_ref[:, ws, :] = yb.reshape(n_rows, W_BLOCK, FOUR_W)
    for r0 in range(0, n_rows, SUBLANES):
        y_ref[0, r0:r0 + SUBLANES] = p_ref[r0:r0 + SUBLANES].astype(BF16)


def _fourier_grid(u, n_batch, n_rows):
    cc, sc = _cos_sin(GROUP)
    cw, sw = _cos_sin(GRID_W)
    cr, sr = _cos_sin(n_rows)
    cs = _bf16_const(np.concatenate([cc, sc], axis=1) / np.sqrt(GROUP))
    m1 = _bf16_const(np.block([[cw, -sw], [sw, cw]]) / np.sqrt(GRID_W))
    eye = np.eye(W_BLOCK)
    k2 = _bf16_const(np.concatenate([np.kron(cr, eye), -np.kron(sr, eye)], axis=1) / np.sqrt(n_rows))
    seq = n_rows * GRID_W
    nk = n_rows * W_BLOCK
    y = pl.pallas_call(
        functools.partial(_fourier_grid_kernel, n_rows=n_rows),
        out_shape=jax.ShapeDtypeStruct((n_batch, n_rows, GRID_W, FOUR_W), BF16),
        grid=(n_batch,),
        in_specs=[
            pl.BlockSpec((seq, FOUR_W), lambda b: (b, 0)),
            _const_spec((GROUP, 2 * GROUP)),
            _const_spec((2 * GRID_W, 2 * GRID_W)),
            _const_spec((nk, 2 * nk)),
        ],
        out_specs=pl.BlockSpec((1, n_rows, GRID_W, FOUR_W), lambda b: (b, 0, 0, 0)),
        scratch_shapes=[pltpu.VMEM((2, seq, FOUR_W), BF16),
                        pltpu.VMEM((n_rows, GRID_W, FOUR_W), F32),
                        pltpu.VMEM((n_rows, GRID_W, FOUR_W), F32)],
        compiler_params=_params("parallel"),
        name="fourier_grid",
    )(u, cs, m1, k2)
    return y.reshape(n_batch * seq, FOUR_W)


def _hgrn_kernel(*refs, reverse, zero_init, final, n_seq):
    it = iter(refs)
    q_ref, f_ref, v_ref = next(it), next(it), next(it)
    s0_ref = None if zero_init else next(it)
    if final:
        oprev_ref, sfwd_ref, gs_ref, gain_ref = next(it), next(it), next(it), next(it)
    o_ref, sfin_ref, st_ref = next(it), next(it), next(it)

    i = pl.program_id(1)
    c = HGRN_CHUNK
    nv = CHUNK_VREGS

    @pl.when(i == 0)
    def _():
        for sq in range(n_seq):
            for hd in range(N_HEADS):
                if zero_init:
                    st_ref[sq * N_HEADS + hd] = jnp.zeros((HEAD, HEAD), F32)
                else:
                    st_ref[sq * N_HEADS + hd] = s0_ref[sq, 0, 0, hd].T

    row_a = lax.broadcasted_iota(jnp.int32, (c, c), 0)
    row_b = lax.broadcasted_iota(jnp.int32, (c, c), 1)
    tt = _chunk_token_of_row(row_a)
    ss = _chunk_token_of_row(row_b)
    xor = tt ^ ss
    ordered = (tt < ss) if reverse else (tt > ss)
    vreg_levels = [1 << b for b in range(nv.bit_length() - 1)]
    sub_levels = [1 << b for b in range(SUBLANES.bit_length() - 1)]
    level_mask = {h: ordered & (xor >= h) & (xor < 2 * h) for h in vreg_levels + [nv * g for g in sub_levels]}
    diag_mask = row_a == row_b
    sub = lax.broadcasted_iota(jnp.int32, (SUBLANES, HEAD), 0)
    out_bit = 0 if reverse else 1
    if final:
        unperm = _chunk_perm_matrix(transpose=True)

    def vregs(x):
        return [x[SUBLANES * j:SUBLANES * (j + 1)] for j in range(nv)]

    def operand(xs):
        return jnp.concatenate(xs, axis=0).astype(BF16)

    def times(x, m):
        return m if x is None else x * m

    def lanes(hd):
        return slice(hd * HEAD, (hd + 1) * HEAD)

    def level_stage(sq, hd):
        sl = lanes(hd)
        f = f_ref[sq, :, sl]
        q = q_ref[sq, :, sl].astype(F32)
        vb = v_ref[sq, :, sl]
        k = 1.0 - f
        qv, kv = vregs(q), vregs(k)
        p = vregs(f)
        r = [None] * nv
        tot = vregs(f)
        a = jnp.zeros((c, c), F32)
        for h in vreg_levels:
            is_out = [((x // h) & 1) == out_bit for x in range(nv)]
            if h == 1:
                zero = jnp.zeros((SUBLANES, 1), F32)
                adj = [jnp.sum(qv[x] * p[x] * kv[x ^ 1], axis=-1, keepdims=True) if is_out[x] else zero
                       for x in range(nv)]
                a = jnp.where(level_mask[h], jnp.concatenate(adj, axis=0), a)
            else:
                qop = operand([qv[x] * p[x] if is_out[x] else qv[x] for x in range(nv)])
                kop = operand([kv[x] if is_out[x] else times(r[x], kv[x]) for x in range(nv)])
                a = jnp.where(level_mask[h], _dot_nt(qop, kop), a)
            new_tot = []
            for b2 in range(len(tot) // 2):
                lo, hi = tot[2 * b2], tot[2 * b2 + 1]
                lo_rows = range(2 * b2 * h, (2 * b2 + 1) * h)
                hi_rows = range((2 * b2 + 1) * h, (2 * b2 + 2) * h)
                if reverse:
                    for x in lo_rows:
                        p[x] = p[x] * hi
                    for x in hi_rows:
                        r[x] = times(r[x], lo)
                else:
                    for x in hi_rows:
                        p[x] = p[x] * lo
                    for x in lo_rows:
                        r[x] = times(r[x], hi)
                new_tot.append(lo * hi)
            tot = new_tot
        t8 = tot[0]
        for g in sub_levels:
            qop = operand([qv[x] * p[x] for x in range(nv)])
            kop = operand([times(r[x], kv[x]) for x in range(nv)])
            a = jnp.where(level_mask[nv * g], _dot_nt(qop, kop), a)
            hi_half = (sub & g) != 0
            w = jnp.where(hi_half, pltpu.roll(t8, g, 0), pltpu.roll(t8, SUBLANES - g, 0))
            grow_p = hi_half != reverse
            mp = jnp.where(grow_p, w, 1.0)
            mr = jnp.where(grow_p, 1.0, w)
            p = [x * mp for x in p]
            r = [times(x, mr) for x in r]
            t8 = t8 * w
        a = jnp.where(diag_mask, jnp.sum(q * k, axis=-1, keepdims=True), a)
        qdec = operand([qv[x] * p[x] for x in range(nv)])
        kdec = operand([kv[x] * r[x] for x in range(nv)])
        return a.astype(BF16), qdec, kdec, t8[0:1, :], vb

    def state_stage(sq, hd, a, qdec, kdec, total, vb):
        slot = sq * N_HEADS + hd
        st = st_ref[slot]
        o = _dot(a, vb) + _dot_nt(qdec, st.astype(BF16))
        vt = vb.astype(F32).T.astype(BF16)
        st_ref[slot] = st * total + _dot(vt, kdec)
        return o

    def output_stage(sq, hd, o):
        sl = lanes(hd)
        if final:
            o = o + oprev_ref[sq, :, sl]
            o = o * lax.rsqrt(jnp.mean(o * o, axis=-1, keepdims=True) + EPS)
            o = o * gain_ref[:, sl] * gs_ref[sq, :, sl].astype(F32)
            o_ref[sq, :, sl] = _dot(unperm, o.astype(BF16)).astype(BF16)
        else:
            o_ref[sq, :, sl] = o

    units = [(sq, hd) for sq in range(n_seq) for hd in range(N_HEADS)]
    scores, outs = {}, {}
    for step in range(len(units) + 2 * HGRN_STAGE_LAG):
        if step < len(units):
            scores[step] = level_stage(*units[step])
        j = step - HGRN_STAGE_LAG
        if 0 <= j < len(units):
            outs[j] = state_stage(*units[j], *scores.pop(j))
        j = step - 2 * HGRN_STAGE_LAG
        if 0 <= j < len(units):
            output_stage(*units[j], outs.pop(j))

    @pl.when(i == pl.num_programs(1) - 1)
    def _():
        for sq in range(n_seq):
            for hd in range(N_HEADS):
                if final:
                    sfin_ref[sq, 0, hd] = sfwd_ref[sq, hd]
                    sfin_ref[sq, 1, hd] = st_ref[sq * N_HEADS + hd].T
                else:
                    sfin_ref[sq, hd] = st_ref[sq * N_HEADS + hd].T


def _hgrn(q, f, v, *, n_batch, seq, reverse, s0=None, s0_dir=0, final_inputs=None):
    c = HGRN_CHUNK
    ns = HGRN_SEQS
    n_chunks = seq // c
    final = final_inputs is not None

    def tok_map(b, i):
        return (b, (n_chunks - 1 - i) if reverse else i, 0)

    tok = pl.BlockSpec((ns, c, HW), tok_map)
    in_specs = [tok, tok, tok]
    args = [q, f, v]
    if s0 is not None:
        in_specs.append(pl.BlockSpec((ns, 1, 1, N_HEADS, HEAD, HEAD), lambda b, i: (b, 0, s0_dir, 0, 0, 0)))
        args.append(s0)
    state_spec = pl.BlockSpec((ns, N_HEADS, HEAD, HEAD), lambda b, i: (b, 0, 0, 0))
    if final:
        oprev, sfwd, gs, gain = final_inputs
        in_specs += [tok, state_spec, tok, pl.BlockSpec((1, HW), lambda b, i: (0, 0))]
        args += [oprev, sfwd, gs, gain]
        out_shape = [jax.ShapeDtypeStruct((n_batch, seq, HW), BF16),
                     jax.ShapeDtypeStruct((n_batch, 2, N_HEADS, HEAD, HEAD), F32)]
        out_specs = [tok, pl.BlockSpec((ns, 2, N_HEADS, HEAD, HEAD), lambda b, i: (b, 0, 0, 0, 0))]
    else:
        out_shape = [jax.ShapeDtypeStruct((n_batch, seq, HW), F32),
                     jax.ShapeDtypeStruct((n_batch, N_HEADS, HEAD, HEAD), F32)]
        out_specs = [tok, state_spec]
    return pl.pallas_call(
        functools.partial(_hgrn_kernel, reverse=reverse, zero_init=s0 is None, final=final, n_seq=ns),
        out_shape=out_shape,
        grid=(n_batch // ns, n_chunks),
        in_specs=in_specs,
        out_specs=out_specs,
        scratch_shapes=[pltpu.VMEM((ns * N_HEADS, HEAD, HEAD), F32)],
        compiler_params=_params("arbitrary", "arbitrary"),
        name="hgrn2_bwd" if reverse else "hgrn2_fwd",
    )(*args)


def _merge_kernel(x_ref, y_ref, o_ref, mod_ref, npre_ref, npost_ref, wga_ref, wgb_ref, wf_ref, wh_ref, wo_ref,
                  out_ref):
    for rows in _sub_tiles():
        x = x_ref[rows, :]
        h = _mixer_input(x, mod_ref, npre_ref)
        merged = (_sigmoid(_dot(h, wga_ref[...])) * _dot(y_ref[rows, :], wf_ref[...])
                  + _sigmoid(_dot(h, wgb_ref[...])) * _dot(o_ref[rows, :], wh_ref[...]))
        m = _dot(merged.astype(BF16), wo_ref[...])
        out_ref[rows, :] = x + mod_ref[0, 5:6, :] * _rms(m, npost_ref[...])


def _merge(x, y, o, mod, npre, npost, win, wf, wh, wo, *, layer, tiles_per_mod):
    assert MIX_COLS % D_MODEL == 0
    gate_block = MIX_COLS // D_MODEL
    t = x.shape[0]
    tm = TOKEN_TILE

    def tok(n):
        return pl.BlockSpec((tm, n), lambda i: (i, 0))

    return pl.pallas_call(
        _merge_kernel,
        out_shape=jax.ShapeDtypeStruct((t, D_MODEL), F32),
        grid=(t // tm,),
        in_specs=[
            tok(D_MODEL), tok(FOUR_W), tok(HW),
            pl.BlockSpec((1, N_MOD, D_MODEL), lambda i: (i // tiles_per_mod, 0, 0)),
            _const_spec((1, D_MODEL)),
            _const_spec((1, D_MODEL)),
            _slab_spec((None, D_MODEL, D_MODEL), (layer, 0, gate_block)),
            _slab_spec((None, D_MODEL, D_MODEL), (layer, 0, gate_block + 1)),
            _slab_spec((None, FOUR_W, D_MODEL), (layer, 0, 0)),
            _slab_spec((None, HW, D_MODEL), (layer, 0, 0)),
            _slab_spec((None, D_MODEL, D_MODEL), (layer, 0, 0)),
        ],
        out_specs=tok(D_MODEL),
        compiler_params=_params("parallel"),
        name="mixer_merge",
    )(x, y, o, mod, npre, npost, win, win, wf, wh, wo)


def _trunk_layer(x, mod, w, *, layer, n_batch, seq, grid_rows, s0):
    n_tiles = x.shape[0] // TOKEN_TILE
    tiles_per_mod = n_tiles // mod.shape[0]
    x1 = _ffn(x, mod, w["npre"][0], w["npost"][0], w["ffn_in"], w["ffn_out"],
              layer=layer, ffn_index=0, mod_base=0, tiles_per_mod=tiles_per_mod)
    u, q, ff, fb, v, gs = _proj(x1, mod, w["npre"][1], w["lb_logits"], w["w_in"],
                                layer=layer, tiles_per_mod=tiles_per_mod)
    if grid_rows is None:
        y = _fourier_seq(u, seq)
    else:
        y = _fourier_grid(u, n_batch, grid_rows)
    q, ff, fb, v, gs = (z.reshape(n_batch, seq, HW) for z in (q, ff, fb, v, gs))
    o_f, s_f = _hgrn(q, ff, v, n_batch=n_batch, seq=seq, reverse=False, s0=s0, s0_dir=0)
    o, states = _hgrn(q, fb, v, n_batch=n_batch, seq=seq, reverse=True, s0=s0, s0_dir=1,
                      final_inputs=(o_f, s_f, gs, w["hgrn_gain"]))
    x2 = _merge(x1, y, o.reshape(n_batch * seq, HW), mod, w["npre"][1], w["npost"][1], w["w_in"],
                w["w_four"], w["w_hgrn"], w["w_out"], layer=layer, tiles_per_mod=tiles_per_mod)
    out = _ffn(x2, mod, w["npre"][2], w["npost"][2], w["ffn_in"], w["ffn_out"],
               layer=layer, ffn_index=1, mod_base=6, tiles_per_mod=tiles_per_mod)
    return out, states


def kernel(x_prompt, x_sample, state_hgrn, c, c_ctx, w_mod, b_mod, norm_pre, norm_post, ffn_w_in, ffn_w_out,
           w_in, w_four, hgrn_gain, w_hgrn, w_out, lb_logits):
    batch, seq, d = x_prompt.shape
    dec_batch, dec_seq, _ = x_sample.shape
    depth = w_mod.shape[0]
    assert depth == 1 and d == D_MODEL
    rows = dec_seq // GRID_W
    layer = 0

    cond = jnp.concatenate([c_ctx[None, :], c, jnp.zeros((MOD_ROWS - 1 - dec_batch, d), F32)], axis=0)
    mod = _modulation(cond, w_mod[layer], b_mod[layer]).reshape(MOD_ROWS, N_MOD, d)
    mod_ctx = mod[0:1]
    mod_dec = mod[1:1 + dec_batch]

    w = dict(
        npre=norm_pre[layer].reshape(3, 1, d),
        npost=norm_post[layer].reshape(3, 1, d),
        ffn_in=ffn_w_in.astype(BF16),
        ffn_out=ffn_w_out.astype(BF16),
        w_in=w_in.astype(BF16),
        w_four=w_four.astype(BF16),
        w_hgrn=w_hgrn.astype(BF16),
        w_out=w_out.astype(BF16),
        hgrn_gain=hgrn_gain[layer].reshape(1, HW),
        lb_logits=lb_logits[layer:layer + 2],
    )

    y_prompt, states = _trunk_layer(x_prompt.reshape(batch * seq, d), mod_ctx, w, layer=layer,
                                    n_batch=batch, seq=seq, grid_rows=None, s0=None)
    new_state = states[:, None].astype(x_prompt.dtype)

    y_sample, _ = _trunk_layer(x_sample.reshape(dec_batch * dec_seq, d), mod_dec, w, layer=layer,
                               n_batch=dec_batch, seq=dec_seq, grid_rows=rows, s0=state_hgrn)
    return (y_prompt.reshape(batch, seq, d), y_sample.reshape(dec_batch, dec_seq, d), new_state)
```

```python
import functools

import jax
import jax.numpy as jnp
import numpy as np
from jax import lax
from jax.experimental import pallas as pl
from jax.experimental.pallas import tpu as pltpu

D_MODEL = 1024
D_FF = 2816
N_MOD = 9
EPS = 1e-6
GRID_W = 64
N_GROUPS = 4
GROUP = 128
FOUR_W = N_GROUPS * GROUP
N_HEADS = 4
HEAD = 128
HW = N_HEADS * HEAD
MIX_COLS = FOUR_W + 3 * HW + 2 * HW
GATE_COLS = 2 * D_MODEL

F32 = jnp.float32
BF16 = jnp.bfloat16

VMEM_LIMIT_BYTES = 56 * 1024 * 1024
FF_CHUNK = 256
TOKEN_TILE = 1024
SUB_TILE = 512
FFN_TILE = 512
HGRN_CHUNK = 128
HGRN_SEQS = 8
HGRN_STAGE_LAG = 3
MOD_ROWS = 16
MOD_COL_TILE = 1536


def _params(*sem):
    return pltpu.CompilerParams(dimension_semantics=sem, vmem_limit_bytes=VMEM_LIMIT_BYTES)


def _const_spec(shape):
    nd = len(shape)
    return pl.BlockSpec(shape, lambda *_: (0,) * nd, pipeline_mode=pl.Buffered(1))


def _slab_spec(block_shape, index):
    return pl.BlockSpec(block_shape, lambda *_: index, pipeline_mode=pl.Buffered(1))


def _sigmoid(x):
    return 1.0 / (1.0 + jnp.exp(-x))


def _silu(x):
    return x * _sigmoid(x)


def _rms(x, gain):
    return x * lax.rsqrt(jnp.mean(x * x, axis=-1, keepdims=True) + EPS) * gain


def _dot(a, b):
    return jnp.dot(a, b, preferred_element_type=F32)


def _dot_nt(a, b):
    return lax.dot_general(a, b, (((1,), (1,)), ((), ())), preferred_element_type=F32)


def _mod_kernel(cond_ref, w_ref, b_ref, out_ref):
    a = _silu(cond_ref[...]).astype(BF16)
    out_ref[...] = _dot(a, w_ref[...].astype(BF16)) + b_ref[...]


def _modulation(cond, w_mod, b_mod):
    n = w_mod.shape[1]
    return pl.pallas_call(
        _mod_kernel,
        out_shape=jax.ShapeDtypeStruct((MOD_ROWS, n), F32),
        grid=(n // MOD_COL_TILE,),
        in_specs=[
            pl.BlockSpec((MOD_ROWS, D_MODEL), lambda j: (0, 0)),
            pl.BlockSpec((D_MODEL, MOD_COL_TILE), lambda j: (0, j)),
            pl.BlockSpec((1, MOD_COL_TILE), lambda j: (0, j)),
        ],
        out_specs=pl.BlockSpec((MOD_ROWS, MOD_COL_TILE), lambda j: (0, j)),
        compiler_params=_params("parallel"),
        name="modulation",
    )(cond, w_mod, b_mod.reshape(1, n))


def _sub_tiles():
    return [slice(s * SUB_TILE, (s + 1) * SUB_TILE) for s in range(TOKEN_TILE // SUB_TILE)]


def _ffn_kernel(xn_ref, xp_ref, modn_ref, modp_ref, npre_ref, npost_ref, win_ref, wout_ref, out_ref,
                h_a, h_b, acc_a, acc_b, *, mod_base):
    i = pl.program_id(0)
    n_chunks = D_FF // FF_CHUNK

    def prologue(x_ref, mod_ref, h_out):
        sh = mod_ref[0, mod_base:mod_base + 1, :]
        sc = mod_ref[0, mod_base + 1:mod_base + 2, :]
        h_out[...] = (_rms(x_ref[...], npre_ref[...]) * (1.0 + sc) + sh).astype(BF16)

    def chunk(j, h_in, acc):
        lo = j * FF_CHUNK
        h = h_in[...]
        g = _dot(h, win_ref[:, lo:lo + FF_CHUNK])
        u = _dot(h, win_ref[:, D_FF + lo:D_FF + lo + FF_CHUNK])
        part = _dot((_silu(g) * u).astype(BF16), wout_ref[lo:lo + FF_CHUNK, :])
        if j == 0:
            acc[...] = part
        else:
            acc[...] += part

    @pl.when(i == 0)
    def _():
        prologue(xp_ref, modp_ref, h_a)
        chunk(0, h_a, acc_a)
        acc_b[...] = jnp.zeros_like(acc_b)

    def body(h_cur, h_next, acc_cur, acc_other):
        gm = modp_ref[0, mod_base + 2:mod_base + 3, :]
        out_ref[...] = xp_ref[...] + 0.5 * gm * _rms(acc_other[...], npost_ref[...])
        for j in range(1, n_chunks):
            chunk(j, h_cur, acc_cur)
        prologue(xn_ref, modn_ref, h_next)
        chunk(0, h_next, acc_other)

    @pl.when(i % 2 == 0)
    def _():
        body(h_a, h_b, acc_a, acc_b)

    @pl.when(i % 2 == 1)
    def _():
        body(h_b, h_a, acc_b, acc_a)


def _ffn(x, mod, npre, npost, win, wout, *, layer, ffn_index, mod_base, tiles_per_mod):
    t = x.shape[0]
    tm = FFN_TILE
    n = t // tm

    def nxt(i):
        return jnp.minimum(i + 1, n - 1)

    def prv(i):
        return jnp.maximum(i - 1, 0)

    tok = lambda idx: pl.BlockSpec((tm, D_MODEL), lambda i: (idx(i), 0))
    modspec = lambda idx: pl.BlockSpec((1, N_MOD, D_MODEL), lambda i: (idx(i) // tiles_per_mod, 0, 0))
    return pl.pallas_call(
        functools.partial(_ffn_kernel, mod_base=mod_base),
        out_shape=jax.ShapeDtypeStruct((t, D_MODEL), F32),
        grid=(n + 1,),
        in_specs=[
            tok(nxt), tok(prv), modspec(nxt), modspec(prv),
            _const_spec((1, D_MODEL)),
            _const_spec((1, D_MODEL)),
            _slab_spec((None, None, D_MODEL, 2 * D_FF), (layer, ffn_index, 0, 0)),
            _slab_spec((None, None, D_FF, D_MODEL), (layer, ffn_index, 0, 0)),
        ],
        out_specs=tok(prv),
        scratch_shapes=[pltpu.VMEM((tm, D_MODEL), BF16), pltpu.VMEM((tm, D_MODEL), BF16),
                        pltpu.VMEM((tm, D_MODEL), F32), pltpu.VMEM((tm, D_MODEL), F32)],
        compiler_params=_params("arbitrary"),
        name="ffn_half_step",
    )(x, x, mod, mod, npre, npost, win, wout)


SUBLANES = 8
CHUNK_VREGS = HGRN_CHUNK // SUBLANES


def _chunk_token_of_row(row):
    return (row & (SUBLANES - 1)) * CHUNK_VREGS + (row >> 3)


def _chunk_perm_matrix(transpose):
    c = HGRN_CHUNK
    a = lax.broadcasted_iota(jnp.int32, (c, c), 0)
    b = lax.broadcasted_iota(jnp.int32, (c, c), 1)
    hit = (a == _chunk_token_of_row(b)) if transpose else (b == _chunk_token_of_row(a))
    return jnp.where(hit, 1.0, 0.0).astype(BF16)


def _mixer_input(x, mod_ref, npre_ref):
    sh = mod_ref[0, 3:4, :]
    sc = mod_ref[0, 4:5, :]
    return (_rms(x, npre_ref[...]) * (1.0 + sc) + sh).astype(BF16)


def _proj_kernel(x_ref, mod_ref, npre_ref, lbl_ref, win_ref,
                 u_ref, q_ref, ff_ref, fb_ref, v_ref, g_ref):
    l0 = lbl_ref[0]
    l1 = lbl_ref[1]
    mx = jnp.maximum(l0, l1)
    e0 = jnp.exp(l0 - mx)
    e1 = jnp.exp(l1 - mx)
    lb = e0 / (e0 + e1)
    perm = _chunk_perm_matrix(transpose=False)
    c = HGRN_CHUNK

    def cols(lhs, lo, n):
        return _dot(lhs, win_ref[:, lo:lo + n])

    for rows in _sub_tiles():
        h = _mixer_input(x_ref[rows, :], mod_ref, npre_ref)
        hp = jnp.concatenate([_dot(perm, h[j * c:(j + 1) * c]) for j in range(SUB_TILE // c)],
                             axis=0).astype(BF16)
        u_ref[rows, :] = cols(h, 0, FOUR_W).astype(BF16)
        q_ref[rows, :] = _silu(cols(hp, FOUR_W, HW)).astype(BF16)
        ff_ref[rows, :] = lb[0:1, :] + (1.0 - lb[0:1, :]) * _sigmoid(cols(hp, FOUR_W + HW, HW))
        fb_ref[rows, :] = lb[1:2, :] + (1.0 - lb[1:2, :]) * _sigmoid(cols(hp, FOUR_W + 2 * HW, HW))
        v_ref[rows, :] = cols(hp, FOUR_W + 3 * HW, HW).astype(BF16)
        g_ref[rows, :] = _silu(cols(hp, FOUR_W + 4 * HW, HW)).astype(BF16)


def _proj(x, mod, npre, lb_logits, win, *, layer, tiles_per_mod):
    t = x.shape[0]
    tm = TOKEN_TILE

    def tok(n):
        return pl.BlockSpec((tm, n), lambda i: (i, 0))

    shapes = [
        jax.ShapeDtypeStruct((t, FOUR_W), BF16),
        jax.ShapeDtypeStruct((t, HW), BF16),
        jax.ShapeDtypeStruct((t, HW), F32),
        jax.ShapeDtypeStruct((t, HW), F32),
        jax.ShapeDtypeStruct((t, HW), BF16),
        jax.ShapeDtypeStruct((t, HW), BF16),
    ]
    return pl.pallas_call(
        _proj_kernel,
        out_shape=shapes,
        grid=(t // tm,),
        in_specs=[
            tok(D_MODEL),
            pl.BlockSpec((1, N_MOD, D_MODEL), lambda i: (i // tiles_per_mod, 0, 0)),
            _const_spec((1, D_MODEL)),
            _const_spec((2, 2, HW)),
            _slab_spec((None, D_MODEL, MIX_COLS), (layer, 0, 0)),
        ],
        out_specs=[tok(FOUR_W), tok(HW), tok(HW), tok(HW), tok(HW), tok(HW)],
        compiler_params=_params("parallel"),
        name="mixer_in_proj",
    )(x, mod, npre, lb_logits, win)


def _cos_sin(n):
    k = np.arange(n)
    ang = 2.0 * np.pi * ((k[:, None] * k[None, :]) % n) / n
    return np.cos(ang), np.sin(ang)


def _bf16_const(a):
    return jnp.asarray(a, F32).astype(BF16)


def _channel_dft(u):
    def go(cs_ref):
        a_parts, b_parts = [], []
        for g in range(N_GROUPS):
            ab = _dot(u[:, g * GROUP:(g + 1) * GROUP], cs_ref[...])
            a_parts.append(ab[:, :GROUP])
            b_parts.append(ab[:, GROUP:])
        return (jnp.concatenate(a_parts, axis=1).astype(BF16),
                jnp.concatenate(b_parts, axis=1).astype(BF16))
    return go


def _fourier_seq_kernel(u_ref, cs_ref, pos_ref, y_ref, *, seq, n_seq):
    for s in range(n_seq):
        a, b = _channel_dft(u_ref[s * seq:(s + 1) * seq, :])(cs_ref)
        ab = jnp.concatenate([a, b], axis=0)
        y_ref[s * seq:(s + 1) * seq, :] = _dot(pos_ref[...], ab).astype(BF16)


def _fourier_seq(u, seq):
    t = u.shape[0]
    n_seq = 4
    cc, sc = _cos_sin(GROUP)
    cl, sl = _cos_sin(seq)
    cs = _bf16_const(np.concatenate([cc, sc], axis=1) / np.sqrt(GROUP))
    pos = _bf16_const(np.concatenate([cl, -sl], axis=1) / np.sqrt(seq))
    rows = n_seq * seq
    return pl.pallas_call(
        functools.partial(_fourier_seq_kernel, seq=seq, n_seq=n_seq),
        out_shape=jax.ShapeDtypeStruct((t, FOUR_W), BF16),
        grid=(t // rows,),
        in_specs=[
            pl.BlockSpec((rows, FOUR_W), lambda i: (i, 0)),
            _const_spec((GROUP, 2 * GROUP)),
            _const_spec((seq, 2 * seq)),
        ],
        out_specs=pl.BlockSpec((rows, FOUR_W), lambda i: (i, 0)),
        compiler_params=_params("parallel"),
        name="fourier_seq",
    )(u, cs, pos)


W_BLOCK = 8
CHANNEL_DFT_ROWS = 512


def _fourier_grid_kernel(u_ref, cs_ref, m1_ref, k2_ref, y_ref, ab_ref, p_ref, q_ref, *, n_rows):
    seq = n_rows * GRID_W
    for blk in range(seq // CHANNEL_DFT_ROWS):
        rows = slice(blk * CHANNEL_DFT_ROWS, (blk + 1) * CHANNEL_DFT_ROWS)
        a, b = _channel_dft(u_ref[rows, :])(cs_ref)
        ab_ref[0, rows, :] = a
        ab_ref[1, rows, :] = b

    def row_body(r, carry):
        start = pl.multiple_of(r * GRID_W, GRID_W)
        ab = jnp.concatenate([ab_ref[0, pl.ds(start, GRID_W), :], ab_ref[1, pl.ds(start, GRID_W), :]], axis=0)
        pq = _dot(m1_ref[...], ab)
        p_ref[r] = pq[:GRID_W]
        q_ref[r] = pq[GRID_W:]
        return carry

    lax.fori_loop(0, n_rows, row_body, 0, unroll=2)

    for j in range(GRID_W // W_BLOCK):
        ws = slice(j * W_BLOCK, (j + 1) * W_BLOCK)
        pb = p_ref[:, ws, :].reshape(n_rows * W_BLOCK, FOUR_W).astype(BF16)
        qb = q_ref[:, ws, :].reshape(n_rows * W_BLOCK, FOUR_W).astype(BF16)
        yb = _dot(k2_ref[...], jnp.concatenate([pb, qb], axis=0))
        p_ref[:, ws, :] = yb.reshape(n_rows, W_BLOCK, FOUR_W)
    for r0 in range(0, n_rows, SUBLANES):
        y_ref[0, r0:r0 + SUBLANES] = p_ref[r0:r0 + SUBLANES].astype(BF16)


def _fourier_grid(u, n_batch, n_rows):
    cc, sc = _cos_sin(GROUP)
    cw, sw = _cos_sin(GRID_W)
    cr, sr = _cos_sin(n_rows)
    cs = _bf16_const(np.concatenate([cc, sc], axis=1) / np.sqrt(GROUP))
    m1 = _bf16_const(np.block([[cw, -sw], [sw, cw]]) / np.sqrt(GRID_W))
    eye = np.eye(W_BLOCK)
    k2 = _bf16_const(np.concatenate([np.kron(cr, eye), -np.kron(sr, eye)], axis=1) / np.sqrt(n_rows))
    seq = n_rows * GRID_W
    nk = n_rows * W_BLOCK
    y = pl.pallas_call(
        functools.partial(_fourier_grid_kernel, n_rows=n_rows),
        out_shape=jax.ShapeDtypeStruct((n_batch, n_rows, GRID_W, FOUR_W), BF16),
        grid=(n_batch,),
        in_specs=[
            pl.BlockSpec((seq, FOUR_W), lambda b: (b, 0)),
            _const_spec((GROUP, 2 * GROUP)),
            _const_spec((2 * GRID_W, 2 * GRID_W)),
            _const_spec((nk, 2 * nk)),
        ],
        out_specs=pl.BlockSpec((1, n_rows, GRID_W, FOUR_W), lambda b: (b, 0, 0, 0)),
        scratch_shapes=[pltpu.VMEM((2, seq, FOUR_W), BF16),
                        pltpu.VMEM((n_rows, GRID_W, FOUR_W), F32),
                        pltpu.VMEM((n_rows, GRID_W, FOUR_W), F32)],
        compiler_params=_params("parallel"),
        name="fourier_grid",
    )(u, cs, m1, k2)
    return y.reshape(n_batch * seq, FOUR_W)


def _hgrn_kernel(*refs, reverse, zero_init, final, n_seq):
    it = iter(refs)
    q_ref, f_ref, v_ref = next(it), next(it), next(it)
    s0_ref = None if zero_init else next(it)
    if final:
        oprev_ref, sfwd_ref, gs_ref, gain_ref = next(it), next(it), next(it), next(it)
    o_ref, sfin_ref, st_ref = next(it), next(it), next(it)

    i = pl.program_id(1)
    c = HGRN_CHUNK
    nv = CHUNK_VREGS

    @pl.when(i == 0)
    def _():
        for sq in range(n_seq):
            for hd in range(N_HEADS):
                if zero_init:
                    st_ref[sq * N_HEADS + hd] = jnp.zeros((HEAD, HEAD), F32)
                else:
                    st_ref[sq * N_HEADS + hd] = s0_ref[sq, 0, 0, hd].T

    row_a = lax.broadcasted_iota(jnp.int32, (c, c), 0)
    row_b = lax.broadcasted_iota(jnp.int32, (c, c), 1)
    tt = _chunk_token_of_row(row_a)
    ss = _chunk_token_of_row(row_b)
    xor = tt ^ ss
    ordered = (tt < ss) if reverse else (tt > ss)
    vreg_levels = [1 << b for b in range(nv.bit_length() - 1)]
    sub_levels = [1 << b for b in range(SUBLANES.bit_length() - 1)]
    level_mask = {h: ordered & (xor >= h) & (xor < 2 * h) for h in vreg_levels + [nv * g for g in sub_levels]}
    diag_mask = row_a == row_b
    sub = lax.broadcasted_iota(jnp.int32, (SUBLANES, HEAD), 0)
    out_bit = 0 if reverse else 1
    if final:
        unperm = _chunk_perm_matrix(transpose=True)

    def vregs(x):
        return [x[SUBLANES * j:SUBLANES * (j + 1)] for j in range(nv)]

    def operand(xs):
        return jnp.concatenate(xs, axis=0).astype(BF16)

    def times(x, m):
        return m if x is None else x * m

    def lanes(hd):
        return slice(hd * HEAD, (hd + 1) * HEAD)

    def level_stage(sq, hd):
        sl = lanes(hd)
        f = f_ref[sq, :, sl]
        q = q_ref[sq, :, sl].astype(F32)
        vb = v_ref[sq, :, sl]
        k = 1.0 - f
        qv, kv = vregs(q), vregs(k)
        p = vregs(f)
        r = [None] * nv
        tot = vregs(f)
        a = jnp.zeros((c, c), F32)
        for h in vreg_levels:
            is_out = [((x // h) & 1) == out_bit for x in range(nv)]
            if h == 1:
                zero = jnp.zeros((SUBLANES, 1), F32)
                adj = [jnp.sum(qv[x] * p[x] * kv[x ^ 1], axis=-1, keepdims=True) if is_out[x] else zero
                       for x in range(nv)]
                a = jnp.where(level_mask[h], jnp.concatenate(adj, axis=0), a)
            else:
                qop = operand([qv[x] * p[x] if is_out[x] else qv[x] for x in range(nv)])
                kop = operand([kv[x] if is_out[x] else times(r[x], kv[x]) for x in range(nv)])
                a = jnp.where(level_mask[h], _dot_nt(qop, kop), a)
            new_tot = []
            for b2 in range(len(tot) // 2):
                lo, hi = tot[2 * b2], tot[2 * b2 + 1]
                lo_rows = range(2 * b2 * h, (2 * b2 + 1) * h)
                hi_rows = range((2 * b2 + 1) * h, (2 * b2 + 2) * h)
                if reverse:
                    for x in lo_rows:
                        p[x] = p[x] * hi
                    for x in hi_rows:
                        r[x] = times(r[x], lo)
                else:
                    for x in hi_rows:
                        p[x] = p[x] * lo
                    for x in lo_rows:
                        r[x] = times(r[x], hi)
                new_tot.append(lo * hi)
            tot = new_tot
        t8 = tot[0]
        for g in sub_levels:
            qop = operand([qv[x] * p[x] for x in range(nv)])
            kop = operand([times(r[x], kv[x]) for x in range(nv)])
            a = jnp.where(level_mask[nv * g], _dot_nt(qop, kop), a)
            hi_half = (sub & g) != 0
            w = jnp.where(hi_half, pltpu.roll(t8, g, 0), pltpu.roll(t8, SUBLANES - g, 0))
            grow_p = hi_half != reverse
            mp = jnp.where(grow_p, w, 1.0)
            mr = jnp.where(grow_p, 1.0, w)
            p = [x * mp for x in p]
            r = [times(x, mr) for x in r]
            t8 = t8 * w
        a = jnp.where(diag_mask, jnp.sum(q * k, axis=-1, keepdims=True), a)
        qdec = operand([qv[x] * p[x] for x in range(nv)])
        kdec = operand([kv[x] * r[x] for x in range(nv)])
        return a.astype(BF16), qdec, kdec, t8[0:1, :], vb

    def state_stage(sq, hd, a, qdec, kdec, total, vb):
        slot = sq * N_HEADS + hd
        st = st_ref[slot]
        o = _dot(a, vb) + _dot_nt(qdec, st.astype(BF16))
        vt = vb.astype(F32).T.astype(BF16)
        st_ref[slot] = st * total + _dot(vt, kdec)
        return o

    def output_stage(sq, hd, o):
        sl = lanes(hd)
        if final:
            o = o + oprev_ref[sq, :, sl]
            o = o * lax.rsqrt(jnp.mean(o * o, axis=-1, keepdims=True) + EPS)
            o = o * gain_ref[:, sl] * gs_ref[sq, :, sl].astype(F32)
            o_ref[sq, :, sl] = _dot(unperm, o.astype(BF16)).astype(BF16)
        else:
            o_ref[sq, :, sl] = o

    units = [(sq, hd) for sq in range(n_seq) for hd in range(N_HEADS)]
    scores, outs = {}, {}
    for step in range(len(units) + 2 * HGRN_STAGE_LAG):
        if step < len(units):
            scores[step] = level_stage(*units[step])
        j = step - HGRN_STAGE_LAG
        if 0 <= j < len(units):
            outs[j] = state_stage(*units[j], *scores.pop(j))
        j = step - 2 * HGRN_STAGE_LAG
        if 0 <= j < len(units):
            output_stage(*units[j], outs.pop(j))

    @pl.when(i == pl.num_programs(1) - 1)
    def _():
        for sq in range(n_seq):
            for hd in range(N_HEADS):
                if final:
                    sfin_ref[sq, 0, hd] = sfwd_ref[sq, hd]
                    sfin_ref[sq, 1, hd] = st_ref[sq * N_HEADS + hd].T
                else:
                    sfin_ref[sq, hd] = st_ref[sq * N_HEADS + hd].T


def _hgrn(q, f, v, *, n_batch, seq, reverse, s0=None, s0_dir=0, final_inputs=None):
    c = HGRN_CHUNK
    ns = HGRN_SEQS
    n_chunks = seq // c
    final = final_inputs is not None

    def tok_map(b, i):
        return (b, (n_chunks - 1 - i) if reverse else i, 0)

    tok = pl.BlockSpec((ns, c, HW), tok_map)
    in_specs = [tok, tok, tok]
    args = [q, f, v]
    if s0 is not None:
        in_specs.append(pl.BlockSpec((ns, 1, 1, N_HEADS, HEAD, HEAD), lambda b, i: (b, 0, s0_dir, 0, 0, 0)))
        args.append(s0)
    state_spec = pl.BlockSpec((ns, N_HEADS, HEAD, HEAD), lambda b, i: (b, 0, 0, 0))
    if final:
        oprev, sfwd, gs, gain = final_inputs
        in_specs += [tok, state_spec, tok, pl.BlockSpec((1, HW), lambda b, i: (0, 0))]
        args += [oprev, sfwd, gs, gain]
        out_shape = [jax.ShapeDtypeStruct((n_batch, seq, HW), BF16),
                     jax.ShapeDtypeStruct((n_batch, 2, N_HEADS, HEAD, HEAD), F32)]
        out_specs = [tok, pl.BlockSpec((ns, 2, N_HEADS, HEAD, HEAD), lambda b, i: (b, 0, 0, 0, 0))]
    else:
        out_shape = [jax.ShapeDtypeStruct((n_batch, seq, HW), F32),
                     jax.ShapeDtypeStruct((n_batch, N_HEADS, HEAD, HEAD), F32)]
        out_specs = [tok, state_spec]
    return pl.pallas_call(
        functools.partial(_hgrn_kernel, reverse=reverse, zero_init=s0 is None, final=final, n_seq=ns),
        out_shape=out_shape,
        grid=(n_batch // ns, n_chunks),
        in_specs=in_specs,
        out_specs=out_specs,
        scratch_shapes=[pltpu.VMEM((ns * N_HEADS, HEAD, HEAD), F32)],
        compiler_params=_params("arbitrary", "arbitrary"),
        name="hgrn2_bwd" if reverse else "hgrn2_fwd",
    )(*args)


def _merge_kernel(x_ref, y_ref, o_ref, mod_ref, npre_ref, npost_ref, wga_ref, wgb_ref, wf_ref, wh_ref, wo_ref,
                  out_ref):
    for rows in _sub_tiles():
        x = x_ref[rows, :]
        h = _mixer_input(x, mod_ref, npre_ref)
        merged = (_sigmoid(_dot(h, wga_ref[...])) * _dot(y_ref[rows, :], wf_ref[...])
                  + _sigmoid(_dot(h, wgb_ref[...])) * _dot(o_ref[rows, :], wh_ref[...]))
        m = _dot(merged.astype(BF16), wo_ref[...])
        out_ref[rows, :] = x + mod_ref[0, 5:6, :] * _rms(m, npost_ref[...])


def _merge(x, y, o, mod, npre, npost, win, wf, wh, wo, *, layer, tiles_per_mod):
    assert MIX_COLS % D_MODEL == 0
    gate_block = MIX_COLS // D_MODEL
    t = x.shape[0]
    tm = TOKEN_TILE

    def tok(n):
        return pl.BlockSpec((tm, n), lambda i: (i, 0))

    return pl.pallas_call(
        _merge_kernel,
        out_shape=jax.ShapeDtypeStruct((t, D_MODEL), F32),
        grid=(t // tm,),
        in_specs=[
            tok(D_MODEL), tok(FOUR_W), tok(HW),
            pl.BlockSpec((1, N_MOD, D_MODEL), lambda i: (i // tiles_per_mod, 0, 0)),
            _const_spec((1, D_MODEL)),
            _const_spec((1, D_MODEL)),
            _slab_spec((None, D_MODEL, D_MODEL), (layer, 0, gate_block)),
            _slab_spec((None, D_MODEL, D_MODEL), (layer, 0, gate_block + 1)),
            _slab_spec((None, FOUR_W, D_MODEL), (layer, 0, 0)),
            _slab_spec((None, HW, D_MODEL), (layer, 0, 0)),
            _slab_spec((None, D_MODEL, D_MODEL), (layer, 0, 0)),
        ],
        out_specs=tok(D_MODEL),
        compiler_params=_params("parallel"),
        name="mixer_merge",
    )(x, y, o, mod, npre, npost, win, win, wf, wh, wo)


def _trunk_layer(x, mod, w, *, layer, n_batch, seq, grid_rows, s0):
    tiles_per_mod = (x.shape[0] // TOKEN_TILE) // mod.shape[0]
    ffn_tiles_per_mod = (x.shape[0] // FFN_TILE) // mod.shape[0]
    x1 = _ffn(x, mod, w["npre"][0], w["npost"][0], w["ffn_in"], w["ffn_out"],
              layer=layer, ffn_index=0, mod_base=0, tiles_per_mod=ffn_tiles_per_mod)
    u, q, ff, fb, v, gs = _proj(x1, mod, w["npre"][1], w["lb_logits"], w["w_in"],
                                layer=layer, tiles_per_mod=tiles_per_mod)
    if grid_rows is None:
        y = _fourier_seq(u, seq)
    else:
        y = _fourier_grid(u, n_batch, grid_rows)
    q, ff, fb, v, gs = (z.reshape(n_batch, seq, HW) for z in (q, ff, fb, v, gs))
    o_f, s_f = _hgrn(q, ff, v, n_batch=n_batch, seq=seq, reverse=False, s0=s0, s0_dir=0)
    o, states = _hgrn(q, fb, v, n_batch=n_batch, seq=seq, reverse=True, s0=s0, s0_dir=1,
                      final_inputs=(o_f, s_f, gs, w["hgrn_gain"]))
    x2 = _merge(x1, y, o.reshape(n_batch * seq, HW), mod, w["npre"][1], w["npost"][1], w["w_in"],
                w["w_four"], w["w_hgrn"], w["w_out"], layer=layer, tiles_per_mod=tiles_per_mod)
    out = _ffn(x2, mod, w["npre"][2], w["npost"][2], w["ffn_in"], w["ffn_out"],
               layer=layer, ffn_index=1, mod_base=6, tiles_per_mod=ffn_tiles_per_mod)
    return out, states


def kernel(x_prompt, x_sample, state_hgrn, c, c_ctx, w_mod, b_mod, norm_pre, norm_post, ffn_w_in, ffn_w_out,
           w_in, w_four, hgrn_gain, w_hgrn, w_out, lb_logits):
    batch, seq, d = x_prompt.shape
    dec_batch, dec_seq, _ = x_sample.shape
    depth = w_mod.shape[0]
    assert depth == 1 and d == D_MODEL
    rows = dec_seq // GRID_W
    layer = 0

    cond = jnp.concatenate([c_ctx[None, :], c, jnp.zeros((MOD_ROWS - 1 - dec_batch, d), F32)], axis=0)
    mod = _modulation(cond, w_mod[layer], b_mod[layer]).reshape(MOD_ROWS, N_MOD, d)
    mod_ctx = mod[0:1]
    mod_dec = mod[1:1 + dec_batch]

    w = dict(
        npre=norm_pre[layer].reshape(3, 1, d),
        npost=norm_post[layer].reshape(3, 1, d),
        ffn_in=ffn_w_in.astype(BF16),
        ffn_out=ffn_w_out.astype(BF16),
        w_in=w_in.astype(BF16),
        w_four=w_four.astype(BF16),
        w_hgrn=w_hgrn.astype(BF16),
        w_out=w_out.astype(BF16),
        hgrn_gain=hgrn_gain[layer].reshape(1, HW),
        lb_logits=lb_logits[layer:layer + 2],
    )

    y_prompt, states = _trunk_layer(x_prompt.reshape(batch * seq, d), mod_ctx, w, layer=layer,
                                    n_batch=batch, seq=seq, grid_rows=None, s0=None)
    new_state = states[:, None].astype(x_prompt.dtype)

    y_sample, _ = _trunk_layer(x_sample.reshape(dec_batch * dec_seq, d), mod_dec, w, layer=layer,
                               n_batch=dec_batch, seq=dec_seq, grid_rows=rows, s0=state_hgrn)
    return (y_prompt.reshape(batch, seq, d), y_sample.reshape(dec_batch, dec_seq, d), new_state)
```

```python
import functools

import jax
import jax.numpy as jnp
import numpy as np
from jax import lax
from jax.experimental import pallas as pl
from jax.experimental.pallas import tpu as pltpu

D_MODEL = 1024
D_FF = 2816
N_MOD = 9
EPS = 1e-6
GRID_W = 64
N_GROUPS = 4
GROUP = 128
FOUR_W = N_GROUPS * GROUP
N_HEADS = 4
HEAD = 128
HW = N_HEADS * HEAD
MIX_COLS = FOUR_W + 3 * HW + 2 * HW
GATE_COLS = 2 * D_MODEL

F32 = jnp.float32
BF16 = jnp.bfloat16

VMEM_LIMIT_BYTES = 56 * 1024 * 1024
FF_CHUNK = 256
TOKEN_TILE = 1024
SUB_TILE = 512
HGRN_CHUNK = 128
HGRN_SEQS = 8
HGRN_STAGE_LAG = 3
MOD_ROWS = 16
MOD_COL_TILE = 1536


def _params(*sem):
    return pltpu.CompilerParams(dimension_semantics=sem, vmem_limit_bytes=VMEM_LIMIT_BYTES)


def _const_spec(shape):
    nd = len(shape)
    return pl.BlockSpec(shape, lambda *_: (0,) * nd, pipeline_mode=pl.Buffered(1))


def _slab_spec(block_shape, index):
    return pl.BlockSpec(block_shape, lambda *_: index, pipeline_mode=pl.Buffered(1))


def _sigmoid(x):
    return 1.0 / (1.0 + jnp.exp(-x))


def _silu(x):
    return x * _sigmoid(x)


def _rms(x, gain):
    return x * lax.rsqrt(jnp.mean(x * x, axis=-1, keepdims=True) + EPS) * gain


def _dot(a, b):
    return jnp.dot(a, b, preferred_element_type=F32)


def _dot_nt(a, b):
    return lax.dot_general(a, b, (((1,), (1,)), ((), ())), preferred_element_type=F32)


def _mod_kernel(cond_ref, w_ref, b_ref, out_ref):
    a = _silu(cond_ref[...]).astype(BF16)
    out_ref[...] = _dot(a, w_ref[...].astype(BF16)) + b_ref[...]


def _modulation(cond, w_mod, b_mod):
    n = w_mod.shape[1]
    return pl.pallas_call(
        _mod_kernel,
        out_shape=jax.ShapeDtypeStruct((MOD_ROWS, n), F32),
        grid=(n // MOD_COL_TILE,),
        in_specs=[
            pl.BlockSpec((MOD_ROWS, D_MODEL), lambda j: (0, 0)),
            pl.BlockSpec((D_MODEL, MOD_COL_TILE), lambda j: (0, j)),
            pl.BlockSpec((1, MOD_COL_TILE), lambda j: (0, j)),
        ],
        out_specs=pl.BlockSpec((MOD_ROWS, MOD_COL_TILE), lambda j: (0, j)),
        compiler_params=_params("parallel"),
        name="modulation",
    )(cond, w_mod, b_mod.reshape(1, n))


def _sub_tiles():
    return [slice(s * SUB_TILE, (s + 1) * SUB_TILE) for s in range(TOKEN_TILE // SUB_TILE)]


def _ffn_kernel(x_ref, mod_ref, npre_ref, npost_ref, win_ref, wout_ref, out_ref, h_ref, acc_ref, *, mod_base):
    sh = mod_ref[0, mod_base:mod_base + 1, :]
    sc = mod_ref[0, mod_base + 1:mod_base + 2, :]
    gm = mod_ref[0, mod_base + 2:mod_base + 3, :]
    sub_tiles = _sub_tiles()
    n_chunks = D_FF // FF_CHUNK

    def prologue(s):
        h_ref[s] = (_rms(x_ref[sub_tiles[s], :], npre_ref[...]) * (1.0 + sc) + sh).astype(BF16)

    def epilogue(s):
        rows = sub_tiles[s]
        out_ref[rows, :] = x_ref[rows, :] + 0.5 * gm * _rms(acc_ref[s], npost_ref[...])

    def hidden(s, j):
        lo = j * FF_CHUNK
        h = h_ref[s]
        g = _dot(h, win_ref[:, lo:lo + FF_CHUNK])
        u = _dot(h, win_ref[:, D_FF + lo:D_FF + lo + FF_CHUNK])
        return (_silu(g) * u).astype(BF16)

    def project(s, j, act):
        lo = j * FF_CHUNK
        part = _dot(act, wout_ref[lo:lo + FF_CHUNK, :])
        if j == 0:
            acc_ref[s] = part
        else:
            acc_ref[s] += part

    items = [(s, j) for s in range(len(sub_tiles)) for j in range(n_chunks)]
    prologue(0)
    act = hidden(*items[0])
    for k, (s, j) in enumerate(items):
        nxt = None
        if k + 1 < len(items):
            if items[k + 1][1] == 0:
                prologue(items[k + 1][0])
            nxt = hidden(*items[k + 1])
        project(s, j, act)
        if j == n_chunks - 1:
            epilogue(s)
        act = nxt


def _ffn(x, mod, npre, npost, win, wout, *, layer, ffn_index, mod_base, tiles_per_mod):
    t = x.shape[0]
    tm = TOKEN_TILE
    return pl.pallas_call(
        functools.partial(_ffn_kernel, mod_base=mod_base),
        out_shape=jax.ShapeDtypeStruct((t, D_MODEL), F32),
        grid=(t // tm,),
        in_specs=[
            pl.BlockSpec((tm, D_MODEL), lambda i: (i, 0)),
            pl.BlockSpec((1, N_MOD, D_MODEL), lambda i: (i // tiles_per_mod, 0, 0)),
            _const_spec((1, D_MODEL)),
            _const_spec((1, D_MODEL)),
            _slab_spec((None, None, D_MODEL, 2 * D_FF), (layer, ffn_index, 0, 0)),
            _slab_spec((None, None, D_FF, D_MODEL), (layer, ffn_index, 0, 0)),
        ],
        out_specs=pl.BlockSpec((tm, D_MODEL), lambda i: (i, 0)),
        scratch_shapes=[pltpu.VMEM((tm // SUB_TILE, SUB_TILE, D_MODEL), BF16),
                        pltpu.VMEM((tm // SUB_TILE, SUB_TILE, D_MODEL), F32)],
        compiler_params=_params("parallel"),
        name="ffn_half_step",
    )(x, mod, npre, npost, win, wout)


SUBLANES = 8
CHUNK_VREGS = HGRN_CHUNK // SUBLANES


def _chunk_token_of_row(row):
    return (row & (SUBLANES - 1)) * CHUNK_VREGS + (row >> 3)


def _chunk_perm_matrix(transpose):
    c = HGRN_CHUNK
    a = lax.broadcasted_iota(jnp.int32, (c, c), 0)
    b = lax.broadcasted_iota(jnp.int32, (c, c), 1)
    hit = (a == _chunk_token_of_row(b)) if transpose else (b == _chunk_token_of_row(a))
    return jnp.where(hit, 1.0, 0.0).astype(BF16)


def _mixer_input(x, mod_ref, npre_ref):
    sh = mod_ref[0, 3:4, :]
    sc = mod_ref[0, 4:5, :]
    return (_rms(x, npre_ref[...]) * (1.0 + sc) + sh).astype(BF16)


def _proj_kernel(x_ref, mod_ref, npre_ref, lbl_ref, win_ref,
                 u_ref, q_ref, ff_ref, fb_ref, v_ref, g_ref):
    l0 = lbl_ref[0]
    l1 = lbl_ref[1]
    mx = jnp.maximum(l0, l1)
    e0 = jnp.exp(l0 - mx)
    e1 = jnp.exp(l1 - mx)
    lb = e0 / (e0 + e1)
    perm = _chunk_perm_matrix(transpose=False)
    c = HGRN_CHUNK

    def cols(lhs, lo, n):
        return _dot(lhs, win_ref[:, lo:lo + n])

    for rows in _sub_tiles():
        h = _mixer_input(x_ref[rows, :], mod_ref, npre_ref)
        hp = jnp.concatenate([_dot(perm, h[j * c:(j + 1) * c]) for j in range(SUB_TILE // c)],
                             axis=0).astype(BF16)
        u_ref[rows, :] = cols(h, 0, FOUR_W).astype(BF16)
        q_ref[rows, :] = _silu(cols(hp, FOUR_W, HW)).astype(BF16)
        ff_ref[rows, :] = lb[0:1, :] + (1.0 - lb[0:1, :]) * _sigmoid(cols(hp, FOUR_W + HW, HW))
        fb_ref[rows, :] = lb[1:2, :] + (1.0 - lb[1:2, :]) * _sigmoid(cols(hp, FOUR_W + 2 * HW, HW))
        v_ref[rows, :] = cols(hp, FOUR_W + 3 * HW, HW).astype(BF16)
        g_ref[rows, :] = _silu(cols(hp, FOUR_W + 4 * HW, HW)).astype(BF16)


def _proj(x, mod, npre, lb_logits, win, *, layer, tiles_per_mod):
    t = x.shape[0]
    tm = TOKEN_TILE

    def tok(n):
        return pl.BlockSpec((tm, n), lambda i: (i, 0))

    shapes = [
        jax.ShapeDtypeStruct((t, FOUR_W), BF16),
        jax.ShapeDtypeStruct((t, HW), BF16),
        jax.ShapeDtypeStruct((t, HW), F32),
        jax.ShapeDtypeStruct((t, HW), F32),
        jax.ShapeDtypeStruct((t, HW), BF16),
        jax.ShapeDtypeStruct((t, HW), BF16),
    ]
    return pl.pallas_call(
        _proj_kernel,
        out_shape=shapes,
        grid=(t // tm,),
        in_specs=[
            tok(D_MODEL),
            pl.BlockSpec((1, N_MOD, D_MODEL), lambda i: (i // tiles_per_mod, 0, 0)),
            _const_spec((1, D_MODEL)),
            _const_spec((2, 2, HW)),
            _slab_spec((None, D_MODEL, MIX_COLS), (layer, 0, 0)),
        ],
        out_specs=[tok(FOUR_W), tok(HW), tok(HW), tok(HW), tok(HW), tok(HW)],
        compiler_params=_params("parallel"),
        name="mixer_in_proj",
    )(x, mod, npre, lb_logits, win)


def _cos_sin(n):
    k = np.arange(n)
    ang = 2.0 * np.pi * ((k[:, None] * k[None, :]) % n) / n
    return np.cos(ang), np.sin(ang)


def _bf16_const(a):
    return jnp.asarray(a, F32).astype(BF16)


def _channel_dft(u):
    def go(cs_ref):
        a_parts, b_parts = [], []
        for g in range(N_GROUPS):
            ab = _dot(u[:, g * GROUP:(g + 1) * GROUP], cs_ref[...])
            a_parts.append(ab[:, :GROUP])
            b_parts.append(ab[:, GROUP:])
        return (jnp.concatenate(a_parts, axis=1).astype(BF16),
                jnp.concatenate(b_parts, axis=1).astype(BF16))
    return go


def _fourier_seq_kernel(u_ref, cs_ref, pos_ref, y_ref, *, seq, n_seq):
    for s in range(n_seq):
        a, b = _channel_dft(u_ref[s * seq:(s + 1) * seq, :])(cs_ref)
        ab = jnp.concatenate([a, b], axis=0)
        y_ref[s * seq:(s + 1) * seq, :] = _dot(pos_ref[...], ab).astype(BF16)


def _fourier_seq(u, seq):
    t = u.shape[0]
    n_seq = 4
    cc, sc = _cos_sin(GROUP)
    cl, sl = _cos_sin(seq)
    cs = _bf16_const(np.concatenate([cc, sc], axis=1) / np.sqrt(GROUP))
    pos = _bf16_const(np.concatenate([cl, -sl], axis=1) / np.sqrt(seq))
    rows = n_seq * seq
    return pl.pallas_call(
        functools.partial(_fourier_seq_kernel, seq=seq, n_seq=n_seq),
        out_shape=jax.ShapeDtypeStruct((t, FOUR_W), BF16),
        grid=(t // rows,),
        in_specs=[
            pl.BlockSpec((rows, FOUR_W), lambda i: (i, 0)),
            _const_spec((GROUP, 2 * GROUP)),
            _const_spec((seq, 2 * seq)),
        ],
        out_specs=pl.BlockSpec((rows, FOUR_W), lambda i: (i, 0)),
        compiler_params=_params("parallel"),
        name="fourier_seq",
    )(u, cs, pos)


W_BLOCK = 8
CHANNEL_DFT_ROWS = 512


def _fourier_grid_kernel(u_ref, cs_ref, m1_ref, k2_ref, y_ref, ab_ref, p_ref, q_ref, *, n_rows):
    seq = n_rows * GRID_W
    for blk in range(seq // CHANNEL_DFT_ROWS):
        rows = slice(blk * CHANNEL_DFT_ROWS, (blk + 1) * CHANNEL_DFT_ROWS)
        a, b = _channel_dft(u_ref[rows, :])(cs_ref)
        ab_ref[0, rows, :] = a
        ab_ref[1, rows, :] = b

    def row_body(r, carry):
        start = pl.multiple_of(r * GRID_W, GRID_W)
        ab = jnp.concatenate([ab_ref[0, pl.ds(start, GRID_W), :], ab_ref[1, pl.ds(start, GRID_W), :]], axis=0)
        pq = _dot(m1_ref[...], ab)
        p_ref[r] = pq[:GRID_W]
        q_ref[r] = pq[GRID_W:]
        return carry

    lax.fori_loop(0, n_rows, row_body, 0, unroll=2)

    for j in range(GRID_W // W_BLOCK):
        ws = slice(j * W_BLOCK, (j + 1) * W_BLOCK)
        pb = p_ref[:, ws, :].reshape(n_rows * W_BLOCK, FOUR_W).astype(BF16)
        qb = q_ref[:, ws, :].reshape(n_rows * W_BLOCK, FOUR_W).astype(BF16)
        yb = _dot(k2_ref[...], jnp.concatenate([pb, qb], axis=0))
        p_ref[:, ws, :] = yb.reshape(n_rows, W_BLOCK, FOUR_W)
    for r0 in range(0, n_rows, SUBLANES):
        y_ref[0, r0:r0 + SUBLANES] = p_ref[r0:r0 + SUBLANES].astype(BF16)


def _fourier_grid(u, n_batch, n_rows):
    cc, sc = _cos_sin(GROUP)
    cw, sw = _cos_sin(GRID_W)
    cr, sr = _cos_sin(n_rows)
    cs = _bf16_const(np.concatenate([cc, sc], axis=1) / np.sqrt(GROUP))
    m1 = _bf16_const(np.block([[cw, -sw], [sw, cw]]) / np.sqrt(GRID_W))
    eye = np.eye(W_BLOCK)
    k2 = _bf16_const(np.concatenate([np.kron(cr, eye), -np.kron(sr, eye)], axis=1) / np.sqrt(n_rows))
    seq = n_rows * GRID_W
    nk = n_rows * W_BLOCK
    y = pl.pallas_call(
        functools.partial(_fourier_grid_kernel, n_rows=n_rows),
        out_shape=jax.ShapeDtypeStruct((n_batch, n_rows, GRID_W, FOUR_W), BF16),
        grid=(n_batch,),
        in_specs=[
            pl.BlockSpec((seq, FOUR_W), lambda b: (b, 0)),
            _const_spec((GROUP, 2 * GROUP)),
            _const_spec((2 * GRID_W, 2 * GRID_W)),
            _const_spec((nk, 2 * nk)),
        ],
        out_specs=pl.BlockSpec((1, n_rows, GRID_W, FOUR_W), lambda b: (b, 0, 0, 0)),
        scratch_shapes=[pltpu.VMEM((2, seq, FOUR_W), BF16),
                        pltpu.VMEM((n_rows, GRID_W, FOUR_W), F32),
                        pltpu.VMEM((n_rows, GRID_W, FOUR_W), F32)],
        compiler_params=_params("parallel"),
        name="fourier_grid",
    )(u, cs, m1, k2)
    return y.reshape(n_batch * seq, FOUR_W)


def _hgrn_kernel(*refs, reverse, zero_init, final, n_seq):
    it = iter(refs)
    q_ref, f_ref, v_ref = next(it), next(it), next(it)
    s0_ref = None if zero_init else next(it)
    if final:
        oprev_ref, sfwd_ref, gs_ref, gain_ref = next(it), next(it), next(it), next(it)
    o_ref, sfin_ref, st_ref = next(it), next(it), next(it)

    i = pl.program_id(1)
    c = HGRN_CHUNK
    nv = CHUNK_VREGS

    @pl.when(i == 0)
    def _():
        for sq in range(n_seq):
            for hd in range(N_HEADS):
                if zero_init:
                    st_ref[sq * N_HEADS + hd] = jnp.zeros((HEAD, HEAD), F32)
                else:
                    st_ref[sq * N_HEADS + hd] = s0_ref[sq, 0, 0, hd].T

    row_a = lax.broadcasted_iota(jnp.int32, (c, c), 0)
    row_b = lax.broadcasted_iota(jnp.int32, (c, c), 1)
    tt = _chunk_token_of_row(row_a)
    ss = _chunk_token_of_row(row_b)
    xor = tt ^ ss
    ordered = (tt < ss) if reverse else (tt > ss)
    vreg_levels = [1 << b for b in range(nv.bit_length() - 1)]
    sub_levels = [1 << b for b in range(SUBLANES.bit_length() - 1)]
    level_mask = {h: ordered & (xor >= h) & (xor < 2 * h) for h in vreg_levels + [nv * g for g in sub_levels]}
    diag_mask = row_a == row_b
    sub = lax.broadcasted_iota(jnp.int32, (SUBLANES, HEAD), 0)
    out_bit = 0 if reverse else 1
    if final:
        unperm = _chunk_perm_matrix(transpose=True)

    def vregs(x):
        return [x[SUBLANES * j:SUBLANES * (j + 1)] for j in range(nv)]

    def operand(xs):
        return jnp.concatenate(xs, axis=0).astype(BF16)

    def times(x, m):
        return m if x is None else x * m

    def lanes(hd):
        return slice(hd * HEAD, (hd + 1) * HEAD)

    def level_stage(sq, hd):
        sl = lanes(hd)
        f = f_ref[sq, :, sl]
        q = q_ref[sq, :, sl].astype(F32)
        vb = v_ref[sq, :, sl]
        k = 1.0 - f
        qv, kv = vregs(q), vregs(k)
        p = vregs(f)
        r = [None] * nv
        tot = vregs(f)
        a = jnp.zeros((c, c), F32)
        for h in vreg_levels:
            is_out = [((x // h) & 1) == out_bit for x in range(nv)]
            if h == 1:
                zero = jnp.zeros((SUBLANES, 1), F32)
                adj = [jnp.sum(qv[x] * p[x] * kv[x ^ 1], axis=-1, keepdims=True) if is_out[x] else zero
                       for x in range(nv)]
                a = jnp.where(level_mask[h], jnp.concatenate(adj, axis=0), a)
            else:
                qop = operand([qv[x] * p[x] if is_out[x] else qv[x] for x in range(nv)])
                kop = operand([kv[x] if is_out[x] else times(r[x], kv[x]) for x in range(nv)])
                a = jnp.where(level_mask[h], _dot_nt(qop, kop), a)
            new_tot = []
            for b2 in range(len(tot) // 2):
                lo, hi = tot[2 * b2], tot[2 * b2 + 1]
                lo_rows = range(2 * b2 * h, (2 * b2 + 1) * h)
                hi_rows = range((2 * b2 + 1) * h, (2 * b2 + 2) * h)
                if reverse:
                    for x in lo_rows:
                        p[x] = p[x] * hi
                    for x in hi_rows:
                        r[x] = times(r[x], lo)
                else:
                    for x in hi_rows:
                        p[x] = p[x] * lo
                    for x in lo_rows:
                        r[x] = times(r[x], hi)
                new_tot.append(lo * hi)
            tot = new_tot
        t8 = tot[0]
        for g in sub_levels:
            qop = operand([qv[x] * p[x] for x in range(nv)])
            kop = operand([times(r[x], kv[x]) for x in range(nv)])
            a = jnp.where(level_mask[nv * g], _dot_nt(qop, kop), a)
            hi_half = (sub & g) != 0
            w = jnp.where(hi_half, pltpu.roll(t8, g, 0), pltpu.roll(t8, SUBLANES - g, 0))
            grow_p = hi_half != reverse
            mp = jnp.where(grow_p, w, 1.0)
            mr = jnp.where(grow_p, 1.0, w)
            p = [x * mp for x in p]
            r = [times(x, mr) for x in r]
            t8 = t8 * w
        a = jnp.where(diag_mask, jnp.sum(q * k, axis=-1, keepdims=True), a)
        qdec = operand([qv[x] * p[x] for x in range(nv)])
        kdec = operand([kv[x] * r[x] for x in range(nv)])
        return a.astype(BF16), qdec, kdec, t8[0:1, :], vb

    def state_stage(sq, hd, a, qdec, kdec, total, vb):
        slot = sq * N_HEADS + hd
        st = st_ref[slot]
        o = _dot(a, vb) + _dot_nt(qdec, st.astype(BF16))
        vt = vb.astype(F32).T.astype(BF16)
        st_ref[slot] = st * total + _dot(vt, kdec)
        return o

    def output_stage(sq, hd, o):
        sl = lanes(hd)
        if final:
            o = o + oprev_ref[sq, :, sl]
            o = o * lax.rsqrt(jnp.mean(o * o, axis=-1, keepdims=True) + EPS)
            o = o * gain_ref[:, sl] * gs_ref[sq, :, sl].astype(F32)
            o_ref[sq, :, sl] = _dot(unperm, o.astype(BF16)).astype(BF16)
        else:
            o_ref[sq, :, sl] = o

    units = [(sq, hd) for sq in range(n_seq) for hd in range(N_HEADS)]
    scores, outs = {}, {}
    for step in range(len(units) + 2 * HGRN_STAGE_LAG):
        if step < len(units):
            scores[step] = level_stage(*units[step])
        j = step - HGRN_STAGE_LAG
        if 0 <= j < len(units):
            outs[j] = state_stage(*units[j], *scores.pop(j))
        j = step - 2 * HGRN_STAGE_LAG
        if 0 <= j < len(units):
            output_stage(*units[j], outs.pop(j))

    @pl.when(i == pl.num_programs(1) - 1)
    def _():
        for sq in range(n_seq):
            for hd in range(N_HEADS):
                if final:
                    sfin_ref[sq, 0, hd] = sfwd_ref[sq, hd]
                    sfin_ref[sq, 1, hd] = st_ref[sq * N_HEADS + hd].T
                else:
                    sfin_ref[sq, hd] = st_ref[sq * N_HEADS + hd].T


def _hgrn(q, f, v, *, n_batch, seq, reverse, s0=None, s0_dir=0, final_inputs=None):
    c = HGRN_CHUNK
    ns = HGRN_SEQS
    n_chunks = seq // c
    final = final_inputs is not None

    def tok_map(b, i):
        return (b, (n_chunks - 1 - i) if reverse else i, 0)

    tok = pl.BlockSpec((ns, c, HW), tok_map)
    in_specs = [tok, tok, tok]
    args = [q, f, v]
    if s0 is not None:
        in_specs.append(pl.BlockSpec((ns, 1, 1, N_HEADS, HEAD, HEAD), lambda b, i: (b, 0, s0_dir, 0, 0, 0)))
        args.append(s0)
    state_spec = pl.BlockSpec((ns, N_HEADS, HEAD, HEAD), lambda b, i: (b, 0, 0, 0))
    if final:
        oprev, sfwd, gs, gain = final_inputs
        in_specs += [tok, state_spec, tok, pl.BlockSpec((1, HW), lambda b, i: (0, 0))]
        args += [oprev, sfwd, gs, gain]
        out_shape = [jax.ShapeDtypeStruct((n_batch, seq, HW), BF16),
                     jax.ShapeDtypeStruct((n_batch, 2, N_HEADS, HEAD, HEAD), F32)]
        out_specs = [tok, pl.BlockSpec((ns, 2, N_HEADS, HEAD, HEAD), lambda b, i: (b, 0, 0, 0, 0))]
    else:
        out_shape = [jax.ShapeDtypeStruct((n_batch, seq, HW), F32),
                     jax.ShapeDtypeStruct((n_batch, N_HEADS, HEAD, HEAD), F32)]
        out_specs = [tok, state_spec]
    return pl.pallas_call(
        functools.partial(_hgrn_kernel, reverse=reverse, zero_init=s0 is None, final=final, n_seq=ns),
        out_shape=out_shape,
        grid=(n_batch // ns, n_chunks),
        in_specs=in_specs,
        out_specs=out_specs,
        scratch_shapes=[pltpu.VMEM((ns * N_HEADS, HEAD, HEAD), F32)],
        compiler_params=_params("arbitrary", "arbitrary"),
        name="hgrn2_bwd" if reverse else "hgrn2_fwd",
    )(*args)


def _merge_kernel(x_ref, y_ref, o_ref, mod_ref, npre_ref, npost_ref, wga_ref, wgb_ref, wf_ref, wh_ref, wo_ref,
                  out_ref):
    merged = []
    for rows in _sub_tiles():
        h = _mixer_input(x_ref[rows, :], mod_ref, npre_ref)
        merged.append((_sigmoid(_dot(h, wga_ref[...])) * _dot(y_ref[rows, :], wf_ref[...])
                       + _sigmoid(_dot(h, wgb_ref[...])) * _dot(o_ref[rows, :], wh_ref[...])).astype(BF16))
    for rows, mg in zip(_sub_tiles(), merged):
        m = _dot(mg, wo_ref[...])
        out_ref[rows, :] = x_ref[rows, :] + mod_ref[0, 5:6, :] * _rms(m, npost_ref[...])


def _merge(x, y, o, mod, npre, npost, win, wf, wh, wo, *, layer, tiles_per_mod):
    assert MIX_COLS % D_MODEL == 0
    gate_block = MIX_COLS // D_MODEL
    t = x.shape[0]
    tm = TOKEN_TILE

    def tok(n):
        return pl.BlockSpec((tm, n), lambda i: (i, 0))

    return pl.pallas_call(
        _merge_kernel,
        out_shape=jax.ShapeDtypeStruct((t, D_MODEL), F32),
        grid=(t // tm,),
        in_specs=[
            tok(D_MODEL), tok(FOUR_W), tok(HW),
            pl.BlockSpec((1, N_MOD, D_MODEL), lambda i: (i // tiles_per_mod, 0, 0)),
            _const_spec((1, D_MODEL)),
            _const_spec((1, D_MODEL)),
            _slab_spec((None, D_MODEL, D_MODEL), (layer, 0, gate_block)),
            _slab_spec((None, D_MODEL, D_MODEL), (layer, 0, gate_block + 1)),
            _slab_spec((None, FOUR_W, D_MODEL), (layer, 0, 0)),
            _slab_spec((None, HW, D_MODEL), (layer, 0, 0)),
            _slab_spec((None, D_MODEL, D_MODEL), (layer, 0, 0)),
        ],
        out_specs=tok(D_MODEL),
        compiler_params=_params("parallel"),
        name="mixer_merge",
    )(x, y, o, mod, npre, npost, win, win, wf, wh, wo)


def _trunk_layer(x, mod, w, *, layer, n_batch, seq, grid_rows, s0):
    tiles_per_mod = (x.shape[0] // TOKEN_TILE) // mod.shape[0]
    x1 = _ffn(x, mod, w["npre"][0], w["npost"][0], w["ffn_in"], w["ffn_out"],
              layer=layer, ffn_index=0, mod_base=0, tiles_per_mod=tiles_per_mod)
    u, q, ff, fb, v, gs = _proj(x1, mod, w["npre"][1], w["lb_logits"], w["w_in"],
                                layer=layer, tiles_per_mod=tiles_per_mod)
    if grid_rows is None:
        y = _fourier_seq(u, seq)
    else:
        y = _fourier_grid(u, n_batch, grid_rows)
    q, ff, fb, v, gs = (z.reshape(n_batch, seq, HW) for z in (q, ff, fb, v, gs))
    o_f, s_f = _hgrn(q, ff, v, n_batch=n_batch, seq=seq, reverse=False, s0=s0, s0_dir=0)
    o, states = _hgrn(q, fb, v, n_batch=n_batch, seq=seq, reverse=True, s0=s0, s0_dir=1,
                      final_inputs=(o_f, s_f, gs, w["hgrn_gain"]))
    x2 = _merge(x1, y, o.reshape(n_batch * seq, HW), mod, w["npre"][1], w["npost"][1], w["w_in"],
                w["w_four"], w["w_hgrn"], w["w_out"], layer=layer, tiles_per_mod=tiles_per_mod)
    out = _ffn(x2, mod, w["npre"][2], w["npost"][2], w["ffn_in"], w["ffn_out"],
               layer=layer, ffn_index=1, mod_base=6, tiles_per_mod=tiles_per_mod)
    return out, states


def kernel(x_prompt, x_sample, state_hgrn, c, c_ctx, w_mod, b_mod, norm_pre, norm_post, ffn_w_in, ffn_w_out,
           w_in, w_four, hgrn_gain, w_hgrn, w_out, lb_logits):
    batch, seq, d = x_prompt.shape
    dec_batch, dec_seq, _ = x_sample.shape
    depth = w_mod.shape[0]
    assert depth == 1 and d == D_MODEL
    rows = dec_seq // GRID_W
    layer = 0

    cond = jnp.concatenate([c_ctx[None, :], c, jnp.zeros((MOD_ROWS - 1 - dec_batch, d), F32)], axis=0)
    mod = _modulation(cond, w_mod[layer], b_mod[layer]).reshape(MOD_ROWS, N_MOD, d)
    mod_ctx = mod[0:1]
    mod_dec = mod[1:1 + dec_batch]

    w = dict(
        npre=norm_pre[layer].reshape(3, 1, d),
        npost=norm_post[layer].reshape(3, 1, d),
        ffn_in=ffn_w_in.astype(BF16),
        ffn_out=ffn_w_out.astype(BF16),
        w_in=w_in.astype(BF16),
        w_four=w_four.astype(BF16),
        w_hgrn=w_hgrn.astype(BF16),
        w_out=w_out.astype(BF16),
        hgrn_gain=hgrn_gain[layer].reshape(1, HW),
        lb_logits=lb_logits[layer:layer + 2],
    )

    y_prompt, states = _trunk_layer(x_prompt.reshape(batch * seq, d), mod_ctx, w, layer=layer,
                                    n_batch=batch, seq=seq, grid_rows=None, s0=None)
    new_state = states[:, None].astype(x_prompt.dtype)

    y_sample, _ = _trunk_layer(x_sample.reshape(dec_batch * dec_seq, d), mod_dec, w, layer=layer,
                               n_batch=dec_batch, seq=dec_seq, grid_rows=rows, s0=state_hgrn)
    return (y_prompt.reshape(batch, seq, d), y_sample.reshape(dec_batch, dec_seq, d), new_state)
```

```python
import functools

import jax
import jax.numpy as jnp
import numpy as np
from jax import lax
from jax.experimental import pallas as pl
from jax.experimental.pallas import tpu as pltpu

D_MODEL = 1024
D_FF = 2816
N_MOD = 9
EPS = 1e-6
GRID_W = 64
N_GROUPS = 4
GROUP = 128
FOUR_W = N_GROUPS * GROUP
N_HEADS = 4
HEAD = 128
HW = N_HEADS * HEAD
MIX_COLS = FOUR_W + 3 * HW + 2 * HW
GATE_COLS = 2 * D_MODEL

F32 = jnp.float32
BF16 = jnp.bfloat16

VMEM_LIMIT_BYTES = 56 * 1024 * 1024
FF_CHUNK = 256
TOKEN_TILE = 1024
SUB_TILE = 1024
HGRN_CHUNK = 128
HGRN_SEQS = 8
HGRN_STAGE_LAG = 3
MOD_ROWS = 16
MOD_COL_TILE = 1536


def _params(*sem):
    return pltpu.CompilerParams(dimension_semantics=sem, vmem_limit_bytes=VMEM_LIMIT_BYTES)


def _const_spec(shape):
    nd = len(shape)
    return pl.BlockSpec(shape, lambda *_: (0,) * nd, pipeline_mode=pl.Buffered(1))


def _slab_spec(block_shape, index):
    return pl.BlockSpec(block_shape, lambda *_: index, pipeline_mode=pl.Buffered(1))


def _sigmoid(x):
    return 1.0 / (1.0 + jnp.exp(-x))


def _silu(x):
    return x * _sigmoid(x)


def _rms(x, gain):
    return x * lax.rsqrt(jnp.mean(x * x, axis=-1, keepdims=True) + EPS) * gain


def _dot(a, b):
    return jnp.dot(a, b, preferred_element_type=F32)


def _dot_nt(a, b):
    return lax.dot_general(a, b, (((1,), (1,)), ((), ())), preferred_element_type=F32)


def _mod_kernel(cond_ref, w_ref, b_ref, out_ref):
    a = _silu(cond_ref[...]).astype(BF16)
    out_ref[...] = _dot(a, w_ref[...].astype(BF16)) + b_ref[...]


def _modulation(cond, w_mod, b_mod):
    n = w_mod.shape[1]
    return pl.pallas_call(
        _mod_kernel,
        out_shape=jax.ShapeDtypeStruct((MOD_ROWS, n), F32),
        grid=(n // MOD_COL_TILE,),
        in_specs=[
            pl.BlockSpec((MOD_ROWS, D_MODEL), lambda j: (0, 0)),
            pl.BlockSpec((D_MODEL, MOD_COL_TILE), lambda j: (0, j)),
            pl.BlockSpec((1, MOD_COL_TILE), lambda j: (0, j)),
        ],
        out_specs=pl.BlockSpec((MOD_ROWS, MOD_COL_TILE), lambda j: (0, j)),
        compiler_params=_params("parallel"),
        name="modulation",
    )(cond, w_mod, b_mod.reshape(1, n))


def _sub_tiles():
    return [slice(s * SUB_TILE, (s + 1) * SUB_TILE) for s in range(TOKEN_TILE // SUB_TILE)]


def _ffn_kernel(x_ref, mod_ref, npre_ref, npost_ref, win_ref, wout_ref, out_ref, h_ref, acc_ref, *, mod_base):
    sh = mod_ref[0, mod_base:mod_base + 1, :]
    sc = mod_ref[0, mod_base + 1:mod_base + 2, :]
    gm = mod_ref[0, mod_base + 2:mod_base + 3, :]
    sub_tiles = _sub_tiles()
    n_chunks = D_FF // FF_CHUNK

    def prologue(s):
        h_ref[s] = (_rms(x_ref[sub_tiles[s], :], npre_ref[...]) * (1.0 + sc) + sh).astype(BF16)

    def epilogue(s):
        rows = sub_tiles[s]
        out_ref[rows, :] = x_ref[rows, :] + 0.5 * gm * _rms(acc_ref[s], npost_ref[...])

    def hidden(s, j):
        lo = j * FF_CHUNK
        h = h_ref[s]
        g = _dot(h, win_ref[:, lo:lo + FF_CHUNK])
        u = _dot(h, win_ref[:, D_FF + lo:D_FF + lo + FF_CHUNK])
        return (_silu(g) * u).astype(BF16)

    def project(s, j, act):
        lo = j * FF_CHUNK
        part = _dot(act, wout_ref[lo:lo + FF_CHUNK, :])
        if j == 0:
            acc_ref[s] = part
        else:
            acc_ref[s] += part

    items = [(s, j) for s in range(len(sub_tiles)) for j in range(n_chunks)]
    prologue(0)
    act = hidden(*items[0])
    for k, (s, j) in enumerate(items):
        nxt = None
        if k + 1 < len(items):
            if items[k + 1][1] == 0:
                prologue(items[k + 1][0])
            nxt = hidden(*items[k + 1])
        project(s, j, act)
        if j == n_chunks - 1:
            epilogue(s)
        act = nxt


def _ffn(x, mod, npre, npost, win, wout, *, layer, ffn_index, mod_base, tiles_per_mod):
    t = x.shape[0]
    tm = TOKEN_TILE
    return pl.pallas_call(
        functools.partial(_ffn_kernel, mod_base=mod_base),
        out_shape=jax.ShapeDtypeStruct((t, D_MODEL), F32),
        grid=(t // tm,),
        in_specs=[
            pl.BlockSpec((tm, D_MODEL), lambda i: (i, 0)),
            pl.BlockSpec((1, N_MOD, D_MODEL), lambda i: (i // tiles_per_mod, 0, 0)),
            _const_spec((1, D_MODEL)),
            _const_spec((1, D_MODEL)),
            _slab_spec((None, None, D_MODEL, 2 * D_FF), (layer, ffn_index, 0, 0)),
            _slab_spec((None, None, D_FF, D_MODEL), (layer, ffn_index, 0, 0)),
        ],
        out_specs=pl.BlockSpec((tm, D_MODEL), lambda i: (i, 0)),
        scratch_shapes=[pltpu.VMEM((tm // SUB_TILE, SUB_TILE, D_MODEL), BF16),
                        pltpu.VMEM((tm // SUB_TILE, SUB_TILE, D_MODEL), F32)],
        compiler_params=_params("parallel"),
        name="ffn_half_step",
    )(x, mod, npre, npost, win, wout)


SUBLANES = 8
CHUNK_VREGS = HGRN_CHUNK // SUBLANES


def _chunk_token_of_row(row):
    return (row & (SUBLANES - 1)) * CHUNK_VREGS + (row >> 3)


def _chunk_perm_matrix(transpose):
    c = HGRN_CHUNK
    a = lax.broadcasted_iota(jnp.int32, (c, c), 0)
    b = lax.broadcasted_iota(jnp.int32, (c, c), 1)
    hit = (a == _chunk_token_of_row(b)) if transpose else (b == _chunk_token_of_row(a))
    return jnp.where(hit, 1.0, 0.0).astype(BF16)


def _mixer_input(x, mod_ref, npre_ref):
    sh = mod_ref[0, 3:4, :]
    sc = mod_ref[0, 4:5, :]
    return (_rms(x, npre_ref[...]) * (1.0 + sc) + sh).astype(BF16)


def _proj_kernel(x_ref, mod_ref, npre_ref, lbl_ref, win_ref,
                 u_ref, q_ref, ff_ref, fb_ref, v_ref, g_ref):
    l0 = lbl_ref[0]
    l1 = lbl_ref[1]
    mx = jnp.maximum(l0, l1)
    e0 = jnp.exp(l0 - mx)
    e1 = jnp.exp(l1 - mx)
    lb = e0 / (e0 + e1)
    perm = _chunk_perm_matrix(transpose=False)
    c = HGRN_CHUNK

    def cols(lhs, lo, n):
        return _dot(lhs, win_ref[:, lo:lo + n])

    sub_tiles = _sub_tiles()
    inputs = {}

    def start(s):
        h = _mixer_input(x_ref[sub_tiles[s], :], mod_ref, npre_ref)
        hp = jnp.concatenate([_dot(perm, h[j * c:(j + 1) * c]) for j in range(SUB_TILE // c)],
                             axis=0).astype(BF16)
        inputs[s] = (h, hp)
        u_ref[sub_tiles[s], :] = cols(h, 0, FOUR_W).astype(BF16)

    def column_group(s, k):
        rows = sub_tiles[s]
        hp = inputs[s][1]
        if k == 0:
            q_ref[rows, :] = _silu(cols(hp, FOUR_W, HW)).astype(BF16)
        elif k == 1:
            ff_ref[rows, :] = lb[0:1, :] + (1.0 - lb[0:1, :]) * _sigmoid(cols(hp, FOUR_W + HW, HW))
        elif k == 2:
            fb_ref[rows, :] = lb[1:2, :] + (1.0 - lb[1:2, :]) * _sigmoid(cols(hp, FOUR_W + 2 * HW, HW))
        elif k == 3:
            v_ref[rows, :] = cols(hp, FOUR_W + 3 * HW, HW).astype(BF16)
        else:
            g_ref[rows, :] = _silu(cols(hp, FOUR_W + 4 * HW, HW)).astype(BF16)

    n_groups = 5
    for s in range(len(sub_tiles)):
        if s == 0:
            start(0)
        for k in range(n_groups):
            column_group(s, k)
            if k == n_groups - 3 and s + 1 < len(sub_tiles):
                start(s + 1)


def _proj(x, mod, npre, lb_logits, win, *, layer, tiles_per_mod):
    t = x.shape[0]
    tm = TOKEN_TILE

    def tok(n):
        return pl.BlockSpec((tm, n), lambda i: (i, 0))

    shapes = [
        jax.ShapeDtypeStruct((t, FOUR_W), BF16),
        jax.ShapeDtypeStruct((t, HW), BF16),
        jax.ShapeDtypeStruct((t, HW), F32),
        jax.ShapeDtypeStruct((t, HW), F32),
        jax.ShapeDtypeStruct((t, HW), BF16),
        jax.ShapeDtypeStruct((t, HW), BF16),
    ]
    return pl.pallas_call(
        _proj_kernel,
        out_shape=shapes,
        grid=(t // tm,),
        in_specs=[
            tok(D_MODEL),
            pl.BlockSpec((1, N_MOD, D_MODEL), lambda i: (i // tiles_per_mod, 0, 0)),
            _const_spec((1, D_MODEL)),
            _const_spec((2, 2, HW)),
            _slab_spec((None, D_MODEL, MIX_COLS), (layer, 0, 0)),
        ],
        out_specs=[tok(FOUR_W), tok(HW), tok(HW), tok(HW), tok(HW), tok(HW)],
        compiler_params=_params("parallel"),
        name="mixer_in_proj",
    )(x, mod, npre, lb_logits, win)


def _cos_sin(n):
    k = np.arange(n)
    ang = 2.0 * np.pi * ((k[:, None] * k[None, :]) % n) / n
    return np.cos(ang), np.sin(ang)


def _bf16_const(a):
    return jnp.asarray(a, F32).astype(BF16)


def _channel_dft(u):
    def go(cs_ref):
        a_parts, b_parts = [], []
        for g in range(N_GROUPS):
            ab = _dot(u[:, g * GROUP:(g + 1) * GROUP], cs_ref[...])
            a_parts.append(ab[:, :GROUP])
            b_parts.append(ab[:, GROUP:])
        return (jnp.concatenate(a_parts, axis=1).astype(BF16),
                jnp.concatenate(b_parts, axis=1).astype(BF16))
    return go


def _fourier_seq_kernel(u_ref, cs_ref, pos_ref, y_ref, *, seq, n_seq):
    for s in range(n_seq):
        a, b = _channel_dft(u_ref[s * seq:(s + 1) * seq, :])(cs_ref)
        ab = jnp.concatenate([a, b], axis=0)
        y_ref[s * seq:(s + 1) * seq, :] = _dot(pos_ref[...], ab).astype(BF16)


def _fourier_seq(u, seq):
    t = u.shape[0]
    n_seq = 4
    cc, sc = _cos_sin(GROUP)
    cl, sl = _cos_sin(seq)
    cs = _bf16_const(np.concatenate([cc, sc], axis=1) / np.sqrt(GROUP))
    pos = _bf16_const(np.concatenate([cl, -sl], axis=1) / np.sqrt(seq))
    rows = n_seq * seq
    return pl.pallas_call(
        functools.partial(_fourier_seq_kernel, seq=seq, n_seq=n_seq),
        out_shape=jax.ShapeDtypeStruct((t, FOUR_W), BF16),
        grid=(t // rows,),
        in_specs=[
            pl.BlockSpec((rows, FOUR_W), lambda i: (i, 0)),
            _const_spec((GROUP, 2 * GROUP)),
            _const_spec((seq, 2 * seq)),
        ],
        out_specs=pl.BlockSpec((rows, FOUR_W), lambda i: (i, 0)),
        compiler_params=_params("parallel"),
        name="fourier_seq",
    )(u, cs, pos)


W_BLOCK = 8
CHANNEL_DFT_ROWS = 512


def _fourier_grid_kernel(u_ref, cs_ref, m1_ref, k2_ref, y_ref, ab_ref, p_ref, q_ref, *, n_rows):
    seq = n_rows * GRID_W
    for blk in range(seq // CHANNEL_DFT_ROWS):
        rows = slice(blk * CHANNEL_DFT_ROWS, (blk + 1) * CHANNEL_DFT_ROWS)
        a, b = _channel_dft(u_ref[rows, :])(cs_ref)
        ab_ref[0, rows, :] = a
        ab_ref[1, rows, :] = b

    def row_body(r, carry):
        start = pl.multiple_of(r * GRID_W, GRID_W)
        ab = jnp.concatenate([ab_ref[0, pl.ds(start, GRID_W), :], ab_ref[1, pl.ds(start, GRID_W), :]], axis=0)
        pq = _dot(m1_ref[...], ab)
        p_ref[r] = pq[:GRID_W]
        q_ref[r] = pq[GRID_W:]
        return carry

    lax.fori_loop(0, n_rows, row_body, 0, unroll=2)

    for j in range(GRID_W // W_BLOCK):
        ws = slice(j * W_BLOCK, (j + 1) * W_BLOCK)
        pb = p_ref[:, ws, :].reshape(n_rows * W_BLOCK, FOUR_W).astype(BF16)
        qb = q_ref[:, ws, :].reshape(n_rows * W_BLOCK, FOUR_W).astype(BF16)
        yb = _dot(k2_ref[...], jnp.concatenate([pb, qb], axis=0))
        p_ref[:, ws, :] = yb.reshape(n_rows, W_BLOCK, FOUR_W)
    for r0 in range(0, n_rows, SUBLANES):
        y_ref[0, r0:r0 + SUBLANES] = p_ref[r0:r0 + SUBLANES].astype(BF16)


def _fourier_grid(u, n_batch, n_rows):
    cc, sc = _cos_sin(GROUP)
    cw, sw = _cos_sin(GRID_W)
    cr, sr = _cos_sin(n_rows)
    cs = _bf16_const(np.concatenate([cc, sc], axis=1) / np.sqrt(GROUP))
    m1 = _bf16_const(np.block([[cw, -sw], [sw, cw]]) / np.sqrt(GRID_W))
    eye = np.eye(W_BLOCK)
    k2 = _bf16_const(np.concatenate([np.kron(cr, eye), -np.kron(sr, eye)], axis=1) / np.sqrt(n_rows))
    seq = n_rows * GRID_W
    nk = n_rows * W_BLOCK
    y = pl.pallas_call(
        functools.partial(_fourier_grid_kernel, n_rows=n_rows),
        out_shape=jax.ShapeDtypeStruct((n_batch, n_rows, GRID_W, FOUR_W), BF16),
        grid=(n_batch,),
        in_specs=[
            pl.BlockSpec((seq, FOUR_W), lambda b: (b, 0)),
            _const_spec((GROUP, 2 * GROUP)),
            _const_spec((2 * GRID_W, 2 * GRID_W)),
            _const_spec((nk, 2 * nk)),
        ],
        out_specs=pl.BlockSpec((1, n_rows, GRID_W, FOUR_W), lambda b: (b, 0, 0, 0)),
        scratch_shapes=[pltpu.VMEM((2, seq, FOUR_W), BF16),
                        pltpu.VMEM((n_rows, GRID_W, FOUR_W), F32),
                        pltpu.VMEM((n_rows, GRID_W, FOUR_W), F32)],
        compiler_params=_params("parallel"),
        name="fourier_grid",
    )(u, cs, m1, k2)
    return y.reshape(n_batch * seq, FOUR_W)


def _hgrn_kernel(*refs, reverse, zero_init, final, n_seq):
    it = iter(refs)
    q_ref, f_ref, v_ref = next(it), next(it), next(it)
    s0_ref = None if zero_init else next(it)
    if final:
        oprev_ref, sfwd_ref, gs_ref, gain_ref = next(it), next(it), next(it), next(it)
    o_ref, sfin_ref, st_ref = next(it), next(it), next(it)

    i = pl.program_id(1)
    c = HGRN_CHUNK
    nv = CHUNK_VREGS

    @pl.when(i == 0)
    def _():
        for sq in range(n_seq):
            for hd in range(N_HEADS):
                if zero_init:
                    st_ref[sq * N_HEADS + hd] = jnp.zeros((HEAD, HEAD), F32)
                else:
                    st_ref[sq * N_HEADS + hd] = s0_ref[sq, 0, 0, hd].T

    row_a = lax.broadcasted_iota(jnp.int32, (c, c), 0)
    row_b = lax.broadcasted_iota(jnp.int32, (c, c), 1)
    tt = _chunk_token_of_row(row_a)
    ss = _chunk_token_of_row(row_b)
    xor = tt ^ ss
    ordered = (tt < ss) if reverse else (tt > ss)
    vreg_levels = [1 << b for b in range(nv.bit_length() - 1)]
    sub_levels = [1 << b for b in range(SUBLANES.bit_length() - 1)]
    level_mask = {h: ordered & (xor >= h) & (xor < 2 * h) for h in vreg_levels + [nv * g for g in sub_levels]}
    diag_mask = row_a == row_b
    sub = lax.broadcasted_iota(jnp.int32, (SUBLANES, HEAD), 0)
    out_bit = 0 if reverse else 1
    if final:
        unperm = _chunk_perm_matrix(transpose=True)

    def vregs(x):
        return [x[SUBLANES * j:SUBLANES * (j + 1)] for j in range(nv)]

    def operand(xs):
        return jnp.concatenate(xs, axis=0).astype(BF16)

    def times(x, m):
        return m if x is None else x * m

    def lanes(hd):
        return slice(hd * HEAD, (hd + 1) * HEAD)

    def level_stage(sq, hd):
        sl = lanes(hd)
        f = f_ref[sq, :, sl]
        q = q_ref[sq, :, sl].astype(F32)
        vb = v_ref[sq, :, sl]
        k = 1.0 - f
        qv, kv = vregs(q), vregs(k)
        p = vregs(f)
        r = [None] * nv
        tot = vregs(f)
        a = jnp.zeros((c, c), F32)
        for h in vreg_levels:
            is_out = [((x // h) & 1) == out_bit for x in range(nv)]
            if h == 1:
                zero = jnp.zeros((SUBLANES, 1), F32)
                adj = [jnp.sum(qv[x] * p[x] * kv[x ^ 1], axis=-1, keepdims=True) if is_out[x] else zero
                       for x in range(nv)]
                a = jnp.where(level_mask[h], jnp.concatenate(adj, axis=0), a)
            else:
                qop = operand([qv[x] * p[x] if is_out[x] else qv[x] for x in range(nv)])
                kop = operand([kv[x] if is_out[x] else times(r[x], kv[x]) for x in range(nv)])
                a = jnp.where(level_mask[h], _dot_nt(qop, kop), a)
            new_tot = []
            for b2 in range(len(tot) // 2):
                lo, hi = tot[2 * b2], tot[2 * b2 + 1]
                lo_rows = range(2 * b2 * h, (2 * b2 + 1) * h)
                hi_rows = range((2 * b2 + 1) * h, (2 * b2 + 2) * h)
                if reverse:
                    for x in lo_rows:
                        p[x] = p[x] * hi
                    for x in hi_rows:
                        r[x] = times(r[x], lo)
                else:
                    for x in hi_rows:
                        p[x] = p[x] * lo
                    for x in lo_rows:
                        r[x] = times(r[x], hi)
                new_tot.append(lo * hi)
            tot = new_tot
        t8 = tot[0]
        for g in sub_levels:
            qop = operand([qv[x] * p[x] for x in range(nv)])
            kop = operand([times(r[x], kv[x]) for x in range(nv)])
            a = jnp.where(level_mask[nv * g], _dot_nt(qop, kop), a)
            hi_half = (sub & g) != 0
            w = jnp.where(hi_half, pltpu.roll(t8, g, 0), pltpu.roll(t8, SUBLANES - g, 0))
            grow_p = hi_half != reverse
            mp = jnp.where(grow_p, w, 1.0)
            mr = jnp.where(grow_p, 1.0, w)
            p = [x * mp for x in p]
            r = [times(x, mr) for x in r]
            t8 = t8 * w
        a = jnp.where(diag_mask, jnp.sum(q * k, axis=-1, keepdims=True), a)
        qdec = operand([qv[x] * p[x] for x in range(nv)])
        kdec = operand([kv[x] * r[x] for x in range(nv)])
        return a.astype(BF16), qdec, kdec, t8[0:1, :], vb

    def state_stage(sq, hd, a, qdec, kdec, total, vb):
        slot = sq * N_HEADS + hd
        st = st_ref[slot]
        o = _dot(a, vb) + _dot_nt(qdec, st.astype(BF16))
        vt = vb.astype(F32).T.astype(BF16)
        st_ref[slot] = st * total + _dot(vt, kdec)
        return o

    def output_stage(sq, hd, o):
        sl = lanes(hd)
        if final:
            o = o + oprev_ref[sq, :, sl]
            o = o * lax.rsqrt(jnp.mean(o * o, axis=-1, keepdims=True) + EPS)
            o = o * gain_ref[:, sl] * gs_ref[sq, :, sl].astype(F32)
            o_ref[sq, :, sl] = _dot(unperm, o.astype(BF16)).astype(BF16)
        else:
            o_ref[sq, :, sl] = o

    units = [(sq, hd) for sq in range(n_seq) for hd in range(N_HEADS)]
    scores, outs = {}, {}
    for step in range(len(units) + 2 * HGRN_STAGE_LAG):
        if step < len(units):
            scores[step] = level_stage(*units[step])
        j = step - HGRN_STAGE_LAG
        if 0 <= j < len(units):
            outs[j] = state_stage(*units[j], *scores.pop(j))
        j = step - 2 * HGRN_STAGE_LAG
        if 0 <= j < len(units):
            output_stage(*units[j], outs.pop(j))

    @pl.when(i == pl.num_programs(1) - 1)
    def _():
        for sq in range(n_seq):
            for hd in range(N_HEADS):
                if final:
                    sfin_ref[sq, 0, hd] = sfwd_ref[sq, hd]
                    sfin_ref[sq, 1, hd] = st_ref[sq * N_HEADS + hd].T
                else:
                    sfin_ref[sq, hd] = st_ref[sq * N_HEADS + hd].T


def _hgrn(q, f, v, *, n_batch, seq, reverse, s0=None, s0_dir=0, final_inputs=None):
    c = HGRN_CHUNK
    ns = HGRN_SEQS
    n_chunks = seq // c
    final = final_inputs is not None

    def tok_map(b, i):
        return (b, (n_chunks - 1 - i) if reverse else i, 0)

    tok = pl.BlockSpec((ns, c, HW), tok_map)
    in_specs = [tok, tok, tok]
    args = [q, f, v]
    if s0 is not None:
        in_specs.append(pl.BlockSpec((ns, 1, 1, N_HEADS, HEAD, HEAD), lambda b, i: (b, 0, s0_dir, 0, 0, 0)))
        args.append(s0)
    state_spec = pl.BlockSpec((ns, N_HEADS, HEAD, HEAD), lambda b, i: (b, 0, 0, 0))
    if final:
        oprev, sfwd, gs, gain = final_inputs
        in_specs += [tok, state_spec, tok, pl.BlockSpec((1, HW), lambda b, i: (0, 0))]
        args += [oprev, sfwd, gs, gain]
        out_shape = [jax.ShapeDtypeStruct((n_batch, seq, HW), BF16),
                     jax.ShapeDtypeStruct((n_batch, 2, N_HEADS, HEAD, HEAD), F32)]
        out_specs = [tok, pl.BlockSpec((ns, 2, N_HEADS, HEAD, HEAD), lambda b, i: (b, 0, 0, 0, 0))]
    else:
        out_shape = [jax.ShapeDtypeStruct((n_batch, seq, HW), F32),
                     jax.ShapeDtypeStruct((n_batch, N_HEADS, HEAD, HEAD), F32)]
        out_specs = [tok, state_spec]
    return pl.pallas_call(
        functools.partial(_hgrn_kernel, reverse=reverse, zero_init=s0 is None, final=final, n_seq=ns),
        out_shape=out_shape,
        grid=(n_batch // ns, n_chunks),
        in_specs=in_specs,
        out_specs=out_specs,
        scratch_shapes=[pltpu.VMEM((ns * N_HEADS, HEAD, HEAD), F32)],
        compiler_params=_params("arbitrary", "arbitrary"),
        name="hgrn2_bwd" if reverse else "hgrn2_fwd",
    )(*args)


def _merge_kernel(x_ref, y_ref, o_ref, mod_ref, npre_ref, npost_ref, wga_ref, wgb_ref, wf_ref, wh_ref, wo_ref,
                  out_ref):
    merged = []
    for rows in _sub_tiles():
        h = _mixer_input(x_ref[rows, :], mod_ref, npre_ref)
        merged.append((_sigmoid(_dot(h, wga_ref[...])) * _dot(y_ref[rows, :], wf_ref[...])
                       + _sigmoid(_dot(h, wgb_ref[...])) * _dot(o_ref[rows, :], wh_ref[...])).astype(BF16))
    for rows, mg in zip(_sub_tiles(), merged):
        m = _dot(mg, wo_ref[...])
        out_ref[rows, :] = x_ref[rows, :] + mod_ref[0, 5:6, :] * _rms(m, npost_ref[...])


def _merge(x, y, o, mod, npre, npost, win, wf, wh, wo, *, layer, tiles_per_mod):
    assert MIX_COLS % D_MODEL == 0
    gate_block = MIX_COLS // D_MODEL
    t = x.shape[0]
    tm = TOKEN_TILE

    def tok(n):
        return pl.BlockSpec((tm, n), lambda i: (i, 0))

    return pl.pallas_call(
        _merge_kernel,
        out_shape=jax.ShapeDtypeStruct((t, D_MODEL), F32),
        grid=(t // tm,),
        in_specs=[
            tok(D_MODEL), tok(FOUR_W), tok(HW),
            pl.BlockSpec((1, N_MOD, D_MODEL), lambda i: (i // tiles_per_mod, 0, 0)),
            _const_spec((1, D_MODEL)),
            _const_spec((1, D_MODEL)),
            _slab_spec((None, D_MODEL, D_MODEL), (layer, 0, gate_block)),
            _slab_spec((None, D_MODEL, D_MODEL), (layer, 0, gate_block + 1)),
            _slab_spec((None, FOUR_W, D_MODEL), (layer, 0, 0)),
            _slab_spec((None, HW, D_MODEL), (layer, 0, 0)),
            _slab_spec((None, D_MODEL, D_MODEL), (layer, 0, 0)),
        ],
        out_specs=tok(D_MODEL),
        compiler_params=_params("parallel"),
        name="mixer_merge",
    )(x, y, o, mod, npre, npost, win, win, wf, wh, wo)


def _trunk_layer(x, mod, w, *, layer, n_batch, seq, grid_rows, s0):
    tiles_per_mod = (x.shape[0] // TOKEN_TILE) // mod.shape[0]
    x1 = _ffn(x, mod, w["npre"][0], w["npost"][0], w["ffn_in"], w["ffn_out"],
              layer=layer, ffn_index=0, mod_base=0, tiles_per_mod=tiles_per_mod)
    u, q, ff, fb, v, gs = _proj(x1, mod, w["npre"][1], w["lb_logits"], w["w_in"],
                                layer=layer, tiles_per_mod=tiles_per_mod)
    if grid_rows is None:
        y = _fourier_seq(u, seq)
    else:
        y = _fourier_grid(u, n_batch, grid_rows)
    q, ff, fb, v, gs = (z.reshape(n_batch, seq, HW) for z in (q, ff, fb, v, gs))
    o_f, s_f = _hgrn(q, ff, v, n_batch=n_batch, seq=seq, reverse=False, s0=s0, s0_dir=0)
    o, states = _hgrn(q, fb, v, n_batch=n_batch, seq=seq, reverse=True, s0=s0, s0_dir=1,
                      final_inputs=(o_f, s_f, gs, w["hgrn_gain"]))
    x2 = _merge(x1, y, o.reshape(n_batch * seq, HW), mod, w["npre"][1], w["npost"][1], w["w_in"],
                w["w_four"], w["w_hgrn"], w["w_out"], layer=layer, tiles_per_mod=tiles_per_mod)
    out = _ffn(x2, mod, w["npre"][2], w["npost"][2], w["ffn_in"], w["ffn_out"],
               layer=layer, ffn_index=1, mod_base=6, tiles_per_mod=tiles_per_mod)
    return out, states


def kernel(x_prompt, x_sample, state_hgrn, c, c_ctx, w_mod, b_mod, norm_pre, norm_post, ffn_w_in, ffn_w_out,
           w_in, w_four, hgrn_gain, w_hgrn, w_out, lb_logits):
    batch, seq, d = x_prompt.shape
    dec_batch, dec_seq, _ = x_sample.shape
    depth = w_mod.shape[0]
    assert depth == 1 and d == D_MODEL
    rows = dec_seq // GRID_W
    layer = 0

    cond = jnp.concatenate([c_ctx[None, :], c, jnp.zeros((MOD_ROWS - 1 - dec_batch, d), F32)], axis=0)
    mod = _modulation(cond, w_mod[layer], b_mod[layer]).reshape(MOD_ROWS, N_MOD, d)
    mod_ctx = mod[0:1]
    mod_dec = mod[1:1 + dec_batch]

    w = dict(
        npre=norm_pre[layer].reshape(3, 1, d),
        npost=norm_post[layer].reshape(3, 1, d),
        ffn_in=ffn_w_in.astype(BF16),
        ffn_out=ffn_w_out.astype(BF16),
        w_in=w_in.astype(BF16),
        w_four=w_four.astype(BF16),
        w_hgrn=w_hgrn.astype(BF16),
        w_out=w_out.astype(BF16),
        hgrn_gain=hgrn_gain[layer].reshape(1, HW),
        lb_logits=lb_logits[layer:layer + 2],
    )

    y_prompt, states = _trunk_layer(x_prompt.reshape(batch * seq, d), mod_ctx, w, layer=layer,
                                    n_batch=batch, seq=seq, grid_rows=None, s0=None)
    new_state = states[:, None].astype(x_prompt.dtype)

    y_sample, _ = _trunk_layer(x_sample.reshape(dec_batch * dec_seq, d), mod_dec, w, layer=layer,
                               n_batch=dec_batch, seq=dec_seq, grid_rows=rows, s0=state_hgrn)
    return (y_prompt.reshape(batch, seq, d), y_sample.reshape(dec_batch, dec_seq, d), new_state)
```

```python
import functools

import jax
import jax.numpy as jnp
import numpy as np
from jax import lax
from jax.experimental import pallas as pl
from jax.experimental.pallas import tpu as pltpu

D_MODEL = 1024
D_FF = 2816
N_MOD = 9
EPS = 1e-6
GRID_W = 64
N_GROUPS = 4
GROUP = 128
FOUR_W = N_GROUPS * GROUP
N_HEADS = 4
HEAD = 128
HW = N_HEADS * HEAD
MIX_COLS = FOUR_W + 3 * HW + 2 * HW
GATE_COLS = 2 * D_MODEL

F32 = jnp.float32
BF16 = jnp.bfloat16

VMEM_LIMIT_BYTES = 56 * 1024 * 1024
FF_CHUNK = 256
TOKEN_TILE = 1024
SUB_TILE = 512
PROJ_SUB_TILE = 1024
HGRN_CHUNK = 128
HGRN_SEQS = 8
HGRN_STAGE_LAG = 3
MOD_ROWS = 16
MOD_COL_TILE = 1536


def _params(*sem):
    return pltpu.CompilerParams(dimension_semantics=sem, vmem_limit_bytes=VMEM_LIMIT_BYTES)


def _const_spec(shape):
    nd = len(shape)
    return pl.BlockSpec(shape, lambda *_: (0,) * nd, pipeline_mode=pl.Buffered(1))


def _slab_spec(block_shape, index):
    return pl.BlockSpec(block_shape, lambda *_: index, pipeline_mode=pl.Buffered(1))


def _sigmoid(x):
    return 1.0 / (1.0 + jnp.exp(-x))


def _silu(x):
    return x * _sigmoid(x)


def _rms(x, gain):
    return x * lax.rsqrt(jnp.mean(x * x, axis=-1, keepdims=True) + EPS) * gain


def _dot(a, b):
    return jnp.dot(a, b, preferred_element_type=F32)


def _dot_nt(a, b):
    return lax.dot_general(a, b, (((1,), (1,)), ((), ())), preferred_element_type=F32)


def _mod_kernel(cond_ref, w_ref, b_ref, out_ref):
    a = _silu(cond_ref[...]).astype(BF16)
    out_ref[...] = _dot(a, w_ref[...].astype(BF16)) + b_ref[...]


def _modulation(cond, w_mod, b_mod):
    n = w_mod.shape[1]
    return pl.pallas_call(
        _mod_kernel,
        out_shape=jax.ShapeDtypeStruct((MOD_ROWS, n), F32),
        grid=(n // MOD_COL_TILE,),
        in_specs=[
            pl.BlockSpec((MOD_ROWS, D_MODEL), lambda j: (0, 0)),
            pl.BlockSpec((D_MODEL, MOD_COL_TILE), lambda j: (0, j)),
            pl.BlockSpec((1, MOD_COL_TILE), lambda j: (0, j)),
        ],
        out_specs=pl.BlockSpec((MOD_ROWS, MOD_COL_TILE), lambda j: (0, j)),
        compiler_params=_params("parallel"),
        name="modulation",
    )(cond, w_mod, b_mod.reshape(1, n))


def _sub_tiles(rows=SUB_TILE):
    return [slice(s * rows, (s + 1) * rows) for s in range(TOKEN_TILE // rows)]


def _ffn_kernel(x_ref, mod_ref, npre_ref, npost_ref, win_ref, wout_ref, out_ref, h_ref, acc_ref, *, mod_base):
    sh = mod_ref[0, mod_base:mod_base + 1, :]
    sc = mod_ref[0, mod_base + 1:mod_base + 2, :]
    gm = mod_ref[0, mod_base + 2:mod_base + 3, :]
    sub_tiles = _sub_tiles()
    n_chunks = D_FF // FF_CHUNK

    def prologue(s):
        h_ref[s] = (_rms(x_ref[sub_tiles[s], :], npre_ref[...]) * (1.0 + sc) + sh).astype(BF16)

    def epilogue(s):
        rows = sub_tiles[s]
        out_ref[rows, :] = x_ref[rows, :] + 0.5 * gm * _rms(acc_ref[s], npost_ref[...])

    def hidden(s, j):
        lo = j * FF_CHUNK
        h = h_ref[s]
        g = _dot(h, win_ref[:, lo:lo + FF_CHUNK])
        u = _dot(h, win_ref[:, D_FF + lo:D_FF + lo + FF_CHUNK])
        return (_silu(g) * u).astype(BF16)

    def project(s, j, act):
        lo = j * FF_CHUNK
        part = _dot(act, wout_ref[lo:lo + FF_CHUNK, :])
        if j == 0:
            acc_ref[s] = part
        else:
            acc_ref[s] += part

    items = [(s, j) for s in range(len(sub_tiles)) for j in range(n_chunks)]
    prologue(0)
    act = hidden(*items[0])
    for k, (s, j) in enumerate(items):
        nxt = None
        if k + 1 < len(items):
            if items[k + 1][1] == 0:
                prologue(items[k + 1][0])
            nxt = hidden(*items[k + 1])
        project(s, j, act)
        if j == n_chunks - 1:
            epilogue(s)
        act = nxt


def _ffn(x, mod, npre, npost, win, wout, *, layer, ffn_index, mod_base, tiles_per_mod):
    t = x.shape[0]
    tm = TOKEN_TILE
    return pl.pallas_call(
        functools.partial(_ffn_kernel, mod_base=mod_base),
        out_shape=jax.ShapeDtypeStruct((t, D_MODEL), F32),
        grid=(t // tm,),
        in_specs=[
            pl.BlockSpec((tm, D_MODEL), lambda i: (i, 0)),
            pl.BlockSpec((1, N_MOD, D_MODEL), lambda i: (i // tiles_per_mod, 0, 0)),
            _const_spec((1, D_MODEL)),
            _const_spec((1, D_MODEL)),
            _slab_spec((None, None, D_MODEL, 2 * D_FF), (layer, ffn_index, 0, 0)),
            _slab_spec((None, None, D_FF, D_MODEL), (layer, ffn_index, 0, 0)),
        ],
        out_specs=pl.BlockSpec((tm, D_MODEL), lambda i: (i, 0)),
        scratch_shapes=[pltpu.VMEM((tm // SUB_TILE, SUB_TILE, D_MODEL), BF16),
                        pltpu.VMEM((tm // SUB_TILE, SUB_TILE, D_MODEL), F32)],
        compiler_params=_params("parallel"),
        name="ffn_half_step",
    )(x, mod, npre, npost, win, wout)


SUBLANES = 8
CHUNK_VREGS = HGRN_CHUNK // SUBLANES


def _chunk_token_of_row(row):
    return (row & (SUBLANES - 1)) * CHUNK_VREGS + (row >> 3)


def _chunk_perm_matrix(transpose):
    c = HGRN_CHUNK
    a = lax.broadcasted_iota(jnp.int32, (c, c), 0)
    b = lax.broadcasted_iota(jnp.int32, (c, c), 1)
    hit = (a == _chunk_token_of_row(b)) if transpose else (b == _chunk_token_of_row(a))
    return jnp.where(hit, 1.0, 0.0).astype(BF16)


def _mixer_input(x, mod_ref, npre_ref):
    sh = mod_ref[0, 3:4, :]
    sc = mod_ref[0, 4:5, :]
    return (_rms(x, npre_ref[...]) * (1.0 + sc) + sh).astype(BF16)


def _proj_kernel(x_ref, mod_ref, npre_ref, lbl_ref, win_ref,
                 u_ref, q_ref, ff_ref, fb_ref, v_ref, g_ref):
    l0 = lbl_ref[0]
    l1 = lbl_ref[1]
    mx = jnp.maximum(l0, l1)
    e0 = jnp.exp(l0 - mx)
    e1 = jnp.exp(l1 - mx)
    lb = e0 / (e0 + e1)
    perm = _chunk_perm_matrix(transpose=False)
    c = HGRN_CHUNK

    def cols(lhs, lo, n):
        return _dot(lhs, win_ref[:, lo:lo + n])

    sub_tiles = _sub_tiles(PROJ_SUB_TILE)
    inputs = {}

    def start(s):
        h = _mixer_input(x_ref[sub_tiles[s], :], mod_ref, npre_ref)
        hp = jnp.concatenate([_dot(perm, h[j * c:(j + 1) * c]) for j in range(PROJ_SUB_TILE // c)],
                             axis=0).astype(BF16)
        inputs[s] = (h, hp)
        u_ref[sub_tiles[s], :] = cols(h, 0, FOUR_W).astype(BF16)

    def column_group(s, k):
        rows = sub_tiles[s]
        hp = inputs[s][1]
        if k == 0:
            q_ref[rows, :] = _silu(cols(hp, FOUR_W, HW)).astype(BF16)
        elif k == 1:
            ff_ref[rows, :] = lb[0:1, :] + (1.0 - lb[0:1, :]) * _sigmoid(cols(hp, FOUR_W + HW, HW))
        elif k == 2:
            fb_ref[rows, :] = lb[1:2, :] + (1.0 - lb[1:2, :]) * _sigmoid(cols(hp, FOUR_W + 2 * HW, HW))
        elif k == 3:
            v_ref[rows, :] = cols(hp, FOUR_W + 3 * HW, HW).astype(BF16)
        else:
            g_ref[rows, :] = _silu(cols(hp, FOUR_W + 4 * HW, HW)).astype(BF16)

    n_groups = 5
    for s in range(len(sub_tiles)):
        if s == 0:
            start(0)
        for k in range(n_groups):
            column_group(s, k)
            if k == n_groups - 3 and s + 1 < len(sub_tiles):
                start(s + 1)


def _proj(x, mod, npre, lb_logits, win, *, layer, tiles_per_mod):
    t = x.shape[0]
    tm = TOKEN_TILE

    def tok(n):
        return pl.BlockSpec((tm, n), lambda i: (i, 0))

    shapes = [
        jax.ShapeDtypeStruct((t, FOUR_W), BF16),
        jax.ShapeDtypeStruct((t, HW), BF16),
        jax.ShapeDtypeStruct((t, HW), F32),
        jax.ShapeDtypeStruct((t, HW), F32),
        jax.ShapeDtypeStruct((t, HW), BF16),
        jax.ShapeDtypeStruct((t, HW), BF16),
    ]
    return pl.pallas_call(
        _proj_kernel,
        out_shape=shapes,
        grid=(t // tm,),
        in_specs=[
            tok(D_MODEL),
            pl.BlockSpec((1, N_MOD, D_MODEL), lambda i: (i // tiles_per_mod, 0, 0)),
            _const_spec((1, D_MODEL)),
            _const_spec((2, 2, HW)),
            _slab_spec((None, D_MODEL, MIX_COLS), (layer, 0, 0)),
        ],
        out_specs=[tok(FOUR_W), tok(HW), tok(HW), tok(HW), tok(HW), tok(HW)],
        compiler_params=_params("parallel"),
        name="mixer_in_proj",
    )(x, mod, npre, lb_logits, win)


def _cos_sin(n):
    k = np.arange(n)
    ang = 2.0 * np.pi * ((k[:, None] * k[None, :]) % n) / n
    return np.cos(ang), np.sin(ang)


def _bf16_const(a):
    return jnp.asarray(a, F32).astype(BF16)


def _channel_dft(u):
    def go(cs_ref):
        a_parts, b_parts = [], []
        for g in range(N_GROUPS):
            ab = _dot(u[:, g * GROUP:(g + 1) * GROUP], cs_ref[...])
            a_parts.append(ab[:, :GROUP])
            b_parts.append(ab[:, GROUP:])
        return (jnp.concatenate(a_parts, axis=1).astype(BF16),
                jnp.concatenate(b_parts, axis=1).astype(BF16))
    return go


def _fourier_seq_kernel(u_ref, cs_ref, pos_ref, y_ref, *, seq, n_seq):
    for s in range(n_seq):
        a, b = _channel_dft(u_ref[s * seq:(s + 1) * seq, :])(cs_ref)
        ab = jnp.concatenate([a, b], axis=0)
        y_ref[s * seq:(s + 1) * seq, :] = _dot(pos_ref[...], ab).astype(BF16)


def _fourier_seq(u, seq):
    t = u.shape[0]
    n_seq = 4
    cc, sc = _cos_sin(GROUP)
    cl, sl = _cos_sin(seq)
    cs = _bf16_const(np.concatenate([cc, sc], axis=1) / np.sqrt(GROUP))
    pos = _bf16_const(np.concatenate([cl, -sl], axis=1) / np.sqrt(seq))
    rows = n_seq * seq
    return pl.pallas_call(
        functools.partial(_fourier_seq_kernel, seq=seq, n_seq=n_seq),
        out_shape=jax.ShapeDtypeStruct((t, FOUR_W), BF16),
        grid=(t // rows,),
        in_specs=[
            pl.BlockSpec((rows, FOUR_W), lambda i: (i, 0)),
            _const_spec((GROUP, 2 * GROUP)),
            _const_spec((seq, 2 * seq)),
        ],
        out_specs=pl.BlockSpec((rows, FOUR_W), lambda i: (i, 0)),
        compiler_params=_params("parallel"),
        name="fourier_seq",
    )(u, cs, pos)


W_BLOCK = 8
CHANNEL_DFT_ROWS = 512


def _fourier_grid_kernel(u_ref, cs_ref, m1_ref, k2_ref, y_ref, ab_ref, p_ref, q_ref, *, n_rows):
    seq = n_rows * GRID_W
    for blk in range(seq // CHANNEL_DFT_ROWS):
        rows = slice(blk * CHANNEL_DFT_ROWS, (blk + 1) * CHANNEL_DFT_ROWS)
        a, b = _channel_dft(u_ref[rows, :])(cs_ref)
        ab_ref[0, rows, :] = a
        ab_ref[1, rows, :] = b

    def row_body(r, carry):
        start = pl.multiple_of(r * GRID_W, GRID_W)
        ab = jnp.concatenate([ab_ref[0, pl.ds(start, GRID_W), :], ab_ref[1, pl.ds(start, GRID_W), :]], axis=0)
        pq = _dot(m1_ref[...], ab)
        p_ref[r] = pq[:GRID_W]
        q_ref[r] = pq[GRID_W:]
        return carry

    lax.fori_loop(0, n_rows, row_body, 0, unroll=2)

    for j in range(GRID_W // W_BLOCK):
        ws = slice(j * W_BLOCK, (j + 1) * W_BLOCK)
        pb = p_ref[:, ws, :].reshape(n_rows * W_BLOCK, FOUR_W).astype(BF16)
        qb = q_ref[:, ws, :].reshape(n_rows * W_BLOCK, FOUR_W).astype(BF16)
        yb = _dot(k2_ref[...], jnp.concatenate([pb, qb], axis=0))
        p_ref[:, ws, :] = yb.reshape(n_rows, W_BLOCK, FOUR_W)
    for r0 in range(0, n_rows, SUBLANES):
        y_ref[0, r0:r0 + SUBLANES] = p_ref[r0:r0 + SUBLANES].astype(BF16)


def _fourier_grid(u, n_batch, n_rows):
    cc, sc = _cos_sin(GROUP)
    cw, sw = _cos_sin(GRID_W)
    cr, sr = _cos_sin(n_rows)
    cs = _bf16_const(np.concatenate([cc, sc], axis=1) / np.sqrt(GROUP))
    m1 = _bf16_const(np.block([[cw, -sw], [sw, cw]]) / np.sqrt(GRID_W))
    eye = np.eye(W_BLOCK)
    k2 = _bf16_const(np.concatenate([np.kron(cr, eye), -np.kron(sr, eye)], axis=1) / np.sqrt(n_rows))
    seq = n_rows * GRID_W
    nk = n_rows * W_BLOCK
    y = pl.pallas_call(
        functools.partial(_fourier_grid_kernel, n_rows=n_rows),
        out_shape=jax.ShapeDtypeStruct((n_batch, n_rows, GRID_W, FOUR_W), BF16),
        grid=(n_batch,),
        in_specs=[
            pl.BlockSpec((seq, FOUR_W), lambda b: (b, 0)),
            _const_spec((GROUP, 2 * GROUP)),
            _const_spec((2 * GRID_W, 2 * GRID_W)),
            _const_spec((nk, 2 * nk)),
        ],
        out_specs=pl.BlockSpec((1, n_rows, GRID_W, FOUR_W), lambda b: (b, 0, 0, 0)),
        scratch_shapes=[pltpu.VMEM((2, seq, FOUR_W), BF16),
                        pltpu.VMEM((n_rows, GRID_W, FOUR_W), F32),
                        pltpu.VMEM((n_rows, GRID_W, FOUR_W), F32)],
        compiler_params=_params("parallel"),
        name="fourier_grid",
    )(u, cs, m1, k2)
    return y.reshape(n_batch * seq, FOUR_W)


def _hgrn_kernel(*refs, reverse, zero_init, final, n_seq):
    it = iter(refs)
    q_ref, f_ref, v_ref = next(it), next(it), next(it)
    s0_ref = None if zero_init else next(it)
    if final:
        oprev_ref, sfwd_ref, gs_ref, gain_ref = next(it), next(it), next(it), next(it)
    o_ref, sfin_ref, st_ref = next(it), next(it), next(it)

    i = pl.program_id(1)
    c = HGRN_CHUNK
    nv = CHUNK_VREGS

    @pl.when(i == 0)
    def _():
        for sq in range(n_seq):
            for hd in range(N_HEADS):
                if zero_init:
                    st_ref[sq * N_HEADS + hd] = jnp.zeros((HEAD, HEAD), F32)
                else:
                    st_ref[sq * N_HEADS + hd] = s0_ref[sq, 0, 0, hd].T

    vreg_levels = [1 << b for b in range(nv.bit_length() - 1)]
    sub_levels = [1 << b for b in range(SUBLANES.bit_length() - 1)]
    sub = lax.broadcasted_iota(jnp.int32, (SUBLANES, HEAD), 0)
    out_bit = 0 if reverse else 1
    if final:
        unperm = _chunk_perm_matrix(transpose=True)

    lane = lax.broadcasted_iota(jnp.int32, (SUBLANES, c), 1)
    lane_i = lane & (SUBLANES - 1)
    lane_r = lane >> 3
    sub_c = lax.broadcasted_iota(jnp.int32, (SUBLANES, c), 0)
    same_i = lane_i == sub_c
    block_mask = {}
    for h in vreg_levels[1:]:
        for pair in range(nv // (2 * h)):
            key_lo = 2 * pair * h + (h if reverse else 0)
            block_mask[h, pair] = same_i & (lane_r >= key_lo) & (lane_r < key_lo + h)
    i_xor = lane_i ^ sub_c
    i_ordered = (sub_c < lane_i) if reverse else (sub_c > lane_i)
    sub_mask = {g: i_ordered & (i_xor >= g) & (i_xor < 2 * g) for g in sub_levels}

    def vregs(x):
        return [x[SUBLANES * j:SUBLANES * (j + 1)] for j in range(nv)]

    def operand(xs):
        return jnp.concatenate(xs, axis=0).astype(BF16)

    def times(x, m):
        return m if x is None else x * m

    def lanes(hd):
        return slice(hd * HEAD, (hd + 1) * HEAD)

    def level_stage(sq, hd):
        sl = lanes(hd)
        f = f_ref[sq, :, sl]
        q = q_ref[sq, :, sl].astype(F32)
        vb = v_ref[sq, :, sl]
        k = 1.0 - f
        qv, kv = vregs(q), vregs(k)
        p = vregs(f)
        r = [None] * nv
        tot = vregs(f)
        a = [None] * nv
        adj = {}

        def merge(x, mask, scores):
            rows = scores[SUBLANES * x:SUBLANES * (x + 1)]
            a[x] = jnp.where(mask, rows, 0.0 if a[x] is None else a[x])

        for h in vreg_levels:
            is_out = [((x // h) & 1) == out_bit for x in range(nv)]
            if h == 1:
                for x in range(nv):
                    if is_out[x]:
                        adj[x] = jnp.sum(qv[x] * p[x] * kv[x ^ 1], axis=-1, keepdims=True)
            else:
                qop = operand([qv[x] * p[x] if is_out[x] else qv[x] for x in range(nv)])
                kop = operand([kv[x] if is_out[x] else times(r[x], kv[x]) for x in range(nv)])
                scores = _dot_nt(qop, kop)
                for x in range(nv):
                    if is_out[x]:
                        merge(x, block_mask[h, x // (2 * h)], scores)
            new_tot = []
            for b2 in range(len(tot) // 2):
                lo, hi = tot[2 * b2], tot[2 * b2 + 1]
                lo_rows = range(2 * b2 * h, (2 * b2 + 1) * h)
                hi_rows = range((2 * b2 + 1) * h, (2 * b2 + 2) * h)
                if reverse:
                    for x in lo_rows:
                        p[x] = p[x] * hi
                    for x in hi_rows:
                        r[x] = times(r[x], lo)
                else:
                    for x in hi_rows:
                        p[x] = p[x] * lo
                    for x in lo_rows:
                        r[x] = times(r[x], hi)
                new_tot.append(lo * hi)
            tot = new_tot
        t8 = tot[0]
        for g in sub_levels:
            qop = operand([qv[x] * p[x] for x in range(nv)])
            kop = operand([times(r[x], kv[x]) for x in range(nv)])
            scores = _dot_nt(qop, kop)
            for x in range(nv):
                merge(x, sub_mask[g], scores)
            hi_half = (sub & g) != 0
            w = jnp.where(hi_half, pltpu.roll(t8, g, 0), pltpu.roll(t8, SUBLANES - g, 0))
            grow_p = hi_half != reverse
            mp = jnp.where(grow_p, w, 1.0)
            mr = jnp.where(grow_p, 1.0, w)
            p = [x * mp for x in p]
            r = [times(x, mr) for x in r]
            t8 = t8 * w
        own = [jnp.sum(qv[x] * kv[x], axis=-1, keepdims=True) for x in range(nv)]
        qdec = operand([qv[x] * p[x] for x in range(nv)])
        kdec = operand([kv[x] * r[x] for x in range(nv)])
        return operand(a), qdec, kdec, t8[0:1, :], vb, own, adj

    def state_stage(sq, hd, a, qdec, kdec, total, vb, own, adj):
        slot = sq * N_HEADS + hd
        st = st_ref[slot]
        vf = vb.astype(F32)
        vv = vregs(vf)
        direct = jnp.concatenate(
            [own[x] * vv[x] + (adj[x] * vv[x ^ 1] if x in adj else 0.0) for x in range(nv)], axis=0)
        o = _dot(a, vb) + _dot_nt(qdec, st.astype(BF16)) + direct
        vt = vf.T.astype(BF16)
        st_ref[slot] = st * total + _dot(vt, kdec)
        return o

    def output_stage(sq, hd, o):
        sl = lanes(hd)
        if final:
            o = o + oprev_ref[sq, :, sl]
            o = o * lax.rsqrt(jnp.mean(o * o, axis=-1, keepdims=True) + EPS)
            o = o * gain_ref[:, sl] * gs_ref[sq, :, sl].astype(F32)
            o_ref[sq, :, sl] = _dot(unperm, o.astype(BF16)).astype(BF16)
        else:
            o_ref[sq, :, sl] = o

    units = [(sq, hd) for sq in range(n_seq) for hd in range(N_HEADS)]
    scores, outs = {}, {}
    for step in range(len(units) + 2 * HGRN_STAGE_LAG):
        if step < len(units):
            scores[step] = level_stage(*units[step])
        j = step - HGRN_STAGE_LAG
        if 0 <= j < len(units):
            outs[j] = state_stage(*units[j], *scores.pop(j))
        j = step - 2 * HGRN_STAGE_LAG
        if 0 <= j < len(units):
            output_stage(*units[j], outs.pop(j))

    @pl.when(i == pl.num_programs(1) - 1)
    def _():
        for sq in range(n_seq):
            for hd in range(N_HEADS):
                if final:
                    sfin_ref[sq, 0, hd] = sfwd_ref[sq, hd]
                    sfin_ref[sq, 1, hd] = st_ref[sq * N_HEADS + hd].T
                else:
                    sfin_ref[sq, hd] = st_ref[sq * N_HEADS + hd].T


def _hgrn(q, f, v, *, n_batch, seq, reverse, s0=None, s0_dir=0, final_inputs=None):
    c = HGRN_CHUNK
    ns = HGRN_SEQS
    n_chunks = seq // c
    final = final_inputs is not None

    def tok_map(b, i):
        return (b, (n_chunks - 1 - i) if reverse else i, 0)

    tok = pl.BlockSpec((ns, c, HW), tok_map)
    in_specs = [tok, tok, tok]
    args = [q, f, v]
    if s0 is not None:
        in_specs.append(pl.BlockSpec((ns, 1, 1, N_HEADS, HEAD, HEAD), lambda b, i: (b, 0, s0_dir, 0, 0, 0)))
        args.append(s0)
    state_spec = pl.BlockSpec((ns, N_HEADS, HEAD, HEAD), lambda b, i: (b, 0, 0, 0))
    if final:
        oprev, sfwd, gs, gain = final_inputs
        in_specs += [tok, state_spec, tok, pl.BlockSpec((1, HW), lambda b, i: (0, 0))]
        args += [oprev, sfwd, gs, gain]
        out_shape = [jax.ShapeDtypeStruct((n_batch, seq, HW), BF16),
                     jax.ShapeDtypeStruct((n_batch, 2, N_HEADS, HEAD, HEAD), F32)]
        out_specs = [tok, pl.BlockSpec((ns, 2, N_HEADS, HEAD, HEAD), lambda b, i: (b, 0, 0, 0, 0))]
    else:
        out_shape = [jax.ShapeDtypeStruct((n_batch, seq, HW), F32),
                     jax.ShapeDtypeStruct((n_batch, N_HEADS, HEAD, HEAD), F32)]
        out_specs = [tok, state_spec]
    return pl.pallas_call(
        functools.partial(_hgrn_kernel, reverse=reverse, zero_init=s0 is None, final=final, n_seq=ns),
        out_shape=out_shape,
        grid=(n_batch // ns, n_chunks),
        in_specs=in_specs,
        out_specs=out_specs,
        scratch_shapes=[pltpu.VMEM((ns * N_HEADS, HEAD, HEAD), F32)],
        compiler_params=_params("arbitrary", "arbitrary"),
        name="hgrn2_bwd" if reverse else "hgrn2_fwd",
    )(*args)


def _merge_kernel(x_ref, y_ref, o_ref, mod_ref, npre_ref, npost_ref, wga_ref, wgb_ref, wf_ref, wh_ref, wo_ref,
                  out_ref):
    merged = []
    for rows in _sub_tiles():
        h = _mixer_input(x_ref[rows, :], mod_ref, npre_ref)
        merged.append((_sigmoid(_dot(h, wga_ref[...])) * _dot(y_ref[rows, :], wf_ref[...])
                       + _sigmoid(_dot(h, wgb_ref[...])) * _dot(o_ref[rows, :], wh_ref[...])).astype(BF16))
    for rows, mg in zip(_sub_tiles(), merged):
        m = _dot(mg, wo_ref[...])
        out_ref[rows, :] = x_ref[rows, :] + mod_ref[0, 5:6, :] * _rms(m, npost_ref[...])


def _merge(x, y, o, mod, npre, npost, win, wf, wh, wo, *, layer, tiles_per_mod):
    assert MIX_COLS % D_MODEL == 0
    gate_block = MIX_COLS // D_MODEL
    t = x.shape[0]
    tm = TOKEN_TILE

    def tok(n):
        return pl.BlockSpec((tm, n), lambda i: (i, 0))

    return pl.pallas_call(
        _merge_kernel,
        out_shape=jax.ShapeDtypeStruct((t, D_MODEL), F32),
        grid=(t // tm,),
        in_specs=[
            tok(D_MODEL), tok(FOUR_W), tok(HW),
            pl.BlockSpec((1, N_MOD, D_MODEL), lambda i: (i // tiles_per_mod, 0, 0)),
            _const_spec((1, D_MODEL)),
            _const_spec((1, D_MODEL)),
            _slab_spec((None, D_MODEL, D_MODEL), (layer, 0, gate_block)),
            _slab_spec((None, D_MODEL, D_MODEL), (layer, 0, gate_block + 1)),
            _slab_spec((None, FOUR_W, D_MODEL), (layer, 0, 0)),
            _slab_spec((None, HW, D_MODEL), (layer, 0, 0)),
            _slab_spec((None, D_MODEL, D_MODEL), (layer, 0, 0)),
        ],
        out_specs=tok(D_MODEL),
        compiler_params=_params("parallel"),
        name="mixer_merge",
    )(x, y, o, mod, npre, npost, win, win, wf, wh, wo)


def _trunk_layer(x, mod, w, *, layer, n_batch, seq, grid_rows, s0):
    tiles_per_mod = (x.shape[0] // TOKEN_TILE) // mod.shape[0]
    x1 = _ffn(x, mod, w["npre"][0], w["npost"][0], w["ffn_in"], w["ffn_out"],
              layer=layer, ffn_index=0, mod_base=0, tiles_per_mod=tiles_per_mod)
    u, q, ff, fb, v, gs = _proj(x1, mod, w["npre"][1], w["lb_logits"], w["w_in"],
                                layer=layer, tiles_per_mod=tiles_per_mod)
    if grid_rows is None:
        y = _fourier_seq(u, seq)
    else:
        y = _fourier_grid(u, n_batch, grid_rows)
    q, ff, fb, v, gs = (z.reshape(n_batch, seq, HW) for z in (q, ff, fb, v, gs))
    o_f, s_f = _hgrn(q, ff, v, n_batch=n_batch, seq=seq, reverse=False, s0=s0, s0_dir=0)
    o, states = _hgrn(q, fb, v, n_batch=n_batch, seq=seq, reverse=True, s0=s0, s0_dir=1,
                      final_inputs=(o_f, s_f, gs, w["hgrn_gain"]))
    x2 = _merge(x1, y, o.reshape(n_batch * seq, HW), mod, w["npre"][1], w["npost"][1], w["w_in"],
                w["w_four"], w["w_hgrn"], w["w_out"], layer=layer, tiles_per_mod=tiles_per_mod)
    out = _ffn(x2, mod, w["npre"][2], w["npost"][2], w["ffn_in"], w["ffn_out"],
               layer=layer, ffn_index=1, mod_base=6, tiles_per_mod=tiles_per_mod)
    return out, states


def kernel(x_prompt, x_sample, state_hgrn, c, c_ctx, w_mod, b_mod, norm_pre, norm_post, ffn_w_in, ffn_w_out,
           w_in, w_four, hgrn_gain, w_hgrn, w_out, lb_logits):
    batch, seq, d = x_prompt.shape
    dec_batch, dec_seq, _ = x_sample.shape
    depth = w_mod.shape[0]
    assert depth == 1 and d == D_MODEL
    rows = dec_seq // GRID_W
    layer = 0

    cond = jnp.concatenate([c_ctx[None, :], c, jnp.zeros((MOD_ROWS - 1 - dec_batch, d), F32)], axis=0)
    mod = _modulation(cond, w_mod[layer], b_mod[layer]).reshape(MOD_ROWS, N_MOD, d)
    mod_ctx = mod[0:1]
    mod_dec = mod[1:1 + dec_batch]

    w = dict(
        npre=norm_pre[layer].reshape(3, 1, d),
        npost=norm_post[layer].reshape(3, 1, d),
        ffn_in=ffn_w_in.astype(BF16),
        ffn_out=ffn_w_out.astype(BF16),
        w_in=w_in.astype(BF16),
        w_four=w_four.astype(BF16),
        w_hgrn=w_hgrn.astype(BF16),
        w_out=w_out.astype(BF16),
        hgrn_gain=hgrn_gain[layer].reshape(1, HW),
        lb_logits=lb_logits[layer:layer + 2],
    )

    y_prompt, states = _trunk_layer(x_prompt.reshape(batch * seq, d), mod_ctx, w, layer=layer,
                                    n_batch=batch, seq=seq, grid_rows=None, s0=None)
    new_state = states[:, None].astype(x_prompt.dtype)

    y_sample, _ = _trunk_layer(x_sample.reshape(dec_batch * dec_seq, d), mod_dec, w, layer=layer,
                               n_batch=dec_batch, seq=dec_seq, grid_rows=rows, s0=state_hgrn)
    return (y_prompt.reshape(batch, seq, d), y_sample.reshape(dec_batch, dec_seq, d), new_state)
```

```python
import functools

import jax
import jax.numpy as jnp
import numpy as np
from jax import lax
from jax.experimental import pallas as pl
from jax.experimental.pallas import tpu as pltpu

D_MODEL = 1024
D_FF = 2816
N_MOD = 9
EPS = 1e-6
GRID_W = 64
N_GROUPS = 4
GROUP = 128
FOUR_W = N_GROUPS * GROUP
N_HEADS = 4
HEAD = 128
HW = N_HEADS * HEAD
MIX_COLS = FOUR_W + 3 * HW + 2 * HW
GATE_COLS = 2 * D_MODEL

F32 = jnp.float32
BF16 = jnp.bfloat16

VMEM_LIMIT_BYTES = 56 * 1024 * 1024
FF_CHUNK = 256
TOKEN_TILE = 1024
SUB_TILE = 512
PROJ_SUB_TILE = 1024
HGRN_CHUNK = 128
HGRN_SEQS = 8
SAFE_BLOCK_DECAY = 1e-25
HGRN_STAGE_LAG = 3
MOD_ROWS = 16
MOD_COL_TILE = 1536


def _params(*sem):
    return pltpu.CompilerParams(dimension_semantics=sem, vmem_limit_bytes=VMEM_LIMIT_BYTES)


def _const_spec(shape):
    nd = len(shape)
    return pl.BlockSpec(shape, lambda *_: (0,) * nd, pipeline_mode=pl.Buffered(1))


def _slab_spec(block_shape, index):
    return pl.BlockSpec(block_shape, lambda *_: index, pipeline_mode=pl.Buffered(1))


def _sigmoid(x):
    return 1.0 / (1.0 + jnp.exp(-x))


def _silu(x):
    return x * _sigmoid(x)


def _rms(x, gain):
    return x * lax.rsqrt(jnp.mean(x * x, axis=-1, keepdims=True) + EPS) * gain


def _dot(a, b):
    return jnp.dot(a, b, preferred_element_type=F32)


def _dot_nt(a, b):
    return lax.dot_general(a, b, (((1,), (1,)), ((), ())), preferred_element_type=F32)


def _mod_kernel(cond_ref, w_ref, b_ref, out_ref):
    a = _silu(cond_ref[...]).astype(BF16)
    out_ref[...] = _dot(a, w_ref[...].astype(BF16)) + b_ref[...]


def _modulation(cond, w_mod, b_mod):
    n = w_mod.shape[1]
    return pl.pallas_call(
        _mod_kernel,
        out_shape=jax.ShapeDtypeStruct((MOD_ROWS, n), F32),
        grid=(n // MOD_COL_TILE,),
        in_specs=[
            pl.BlockSpec((MOD_ROWS, D_MODEL), lambda j: (0, 0)),
            pl.BlockSpec((D_MODEL, MOD_COL_TILE), lambda j: (0, j)),
            pl.BlockSpec((1, MOD_COL_TILE), lambda j: (0, j)),
        ],
        out_specs=pl.BlockSpec((MOD_ROWS, MOD_COL_TILE), lambda j: (0, j)),
        compiler_params=_params("parallel"),
        name="modulation",
    )(cond, w_mod, b_mod.reshape(1, n))


def _sub_tiles(rows=SUB_TILE):
    return [slice(s * rows, (s + 1) * rows) for s in range(TOKEN_TILE // rows)]


def _ffn_kernel(x_ref, mod_ref, npre_ref, npost_ref, win_ref, wout_ref, out_ref, h_ref, acc_ref, *, mod_base):
    sh = mod_ref[0, mod_base:mod_base + 1, :]
    sc = mod_ref[0, mod_base + 1:mod_base + 2, :]
    gm = mod_ref[0, mod_base + 2:mod_base + 3, :]
    sub_tiles = _sub_tiles()
    n_chunks = D_FF // FF_CHUNK

    def prologue(s):
        h_ref[s] = (_rms(x_ref[sub_tiles[s], :], npre_ref[...]) * (1.0 + sc) + sh).astype(BF16)

    def epilogue(s):
        rows = sub_tiles[s]
        out_ref[rows, :] = x_ref[rows, :] + 0.5 * gm * _rms(acc_ref[s], npost_ref[...])

    def hidden(s, j):
        lo = j * FF_CHUNK
        h = h_ref[s]
        g = _dot(h, win_ref[:, lo:lo + FF_CHUNK])
        u = _dot(h, win_ref[:, D_FF + lo:D_FF + lo + FF_CHUNK])
        return (_silu(g) * u).astype(BF16)

    def project(s, j, act):
        lo = j * FF_CHUNK
        part = _dot(act, wout_ref[lo:lo + FF_CHUNK, :])
        if j == 0:
            acc_ref[s] = part
        else:
            acc_ref[s] += part

    items = [(s, j) for s in range(len(sub_tiles)) for j in range(n_chunks)]
    prologue(0)
    act = hidden(*items[0])
    for k, (s, j) in enumerate(items):
        nxt = None
        if k + 1 < len(items):
            if items[k + 1][1] == 0:
                prologue(items[k + 1][0])
            nxt = hidden(*items[k + 1])
        project(s, j, act)
        if j == n_chunks - 1:
            epilogue(s)
        act = nxt


def _ffn(x, mod, npre, npost, win, wout, *, layer, ffn_index, mod_base, tiles_per_mod):
    t = x.shape[0]
    tm = TOKEN_TILE
    return pl.pallas_call(
        functools.partial(_ffn_kernel, mod_base=mod_base),
        out_shape=jax.ShapeDtypeStruct((t, D_MODEL), F32),
        grid=(t // tm,),
        in_specs=[
            pl.BlockSpec((tm, D_MODEL), lambda i: (i, 0)),
            pl.BlockSpec((1, N_MOD, D_MODEL), lambda i: (i // tiles_per_mod, 0, 0)),
            _const_spec((1, D_MODEL)),
            _const_spec((1, D_MODEL)),
            _slab_spec((None, None, D_MODEL, 2 * D_FF), (layer, ffn_index, 0, 0)),
            _slab_spec((None, None, D_FF, D_MODEL), (layer, ffn_index, 0, 0)),
        ],
        out_specs=pl.BlockSpec((tm, D_MODEL), lambda i: (i, 0)),
        scratch_shapes=[pltpu.VMEM((tm // SUB_TILE, SUB_TILE, D_MODEL), BF16),
                        pltpu.VMEM((tm // SUB_TILE, SUB_TILE, D_MODEL), F32)],
        compiler_params=_params("parallel"),
        name="ffn_half_step",
    )(x, mod, npre, npost, win, wout)


SUBLANES = 8
CHUNK_VREGS = HGRN_CHUNK // SUBLANES


def _chunk_token_of_row(row):
    return (row & (SUBLANES - 1)) * CHUNK_VREGS + (row >> 3)


def _chunk_perm_matrix(transpose):
    c = HGRN_CHUNK
    a = lax.broadcasted_iota(jnp.int32, (c, c), 0)
    b = lax.broadcasted_iota(jnp.int32, (c, c), 1)
    hit = (a == _chunk_token_of_row(b)) if transpose else (b == _chunk_token_of_row(a))
    return jnp.where(hit, 1.0, 0.0).astype(BF16)


def _mixer_input(x, mod_ref, npre_ref):
    sh = mod_ref[0, 3:4, :]
    sc = mod_ref[0, 4:5, :]
    return (_rms(x, npre_ref[...]) * (1.0 + sc) + sh).astype(BF16)


def _proj_kernel(x_ref, mod_ref, npre_ref, lbl_ref, win_ref,
                 u_ref, q_ref, ff_ref, fb_ref, v_ref, g_ref):
    l0 = lbl_ref[0]
    l1 = lbl_ref[1]
    mx = jnp.maximum(l0, l1)
    e0 = jnp.exp(l0 - mx)
    e1 = jnp.exp(l1 - mx)
    lb = e0 / (e0 + e1)
    perm = _chunk_perm_matrix(transpose=False)
    c = HGRN_CHUNK

    def cols(lhs, lo, n):
        return _dot(lhs, win_ref[:, lo:lo + n])

    sub_tiles = _sub_tiles(PROJ_SUB_TILE)
    inputs = {}

    def start(s):
        h = _mixer_input(x_ref[sub_tiles[s], :], mod_ref, npre_ref)
        hp = jnp.concatenate([_dot(perm, h[j * c:(j + 1) * c]) for j in range(PROJ_SUB_TILE // c)],
                             axis=0).astype(BF16)
        inputs[s] = (h, hp)
        u_ref[sub_tiles[s], :] = cols(h, 0, FOUR_W).astype(BF16)

    def column_group(s, k):
        rows = sub_tiles[s]
        hp = inputs[s][1]
        if k == 0:
            q_ref[rows, :] = _silu(cols(hp, FOUR_W, HW)).astype(BF16)
        elif k == 1:
            ff_ref[rows, :] = lb[0:1, :] + (1.0 - lb[0:1, :]) * _sigmoid(cols(hp, FOUR_W + HW, HW))
        elif k == 2:
            fb_ref[rows, :] = lb[1:2, :] + (1.0 - lb[1:2, :]) * _sigmoid(cols(hp, FOUR_W + 2 * HW, HW))
        elif k == 3:
            v_ref[rows, :] = cols(hp, FOUR_W + 3 * HW, HW).astype(BF16)
        else:
            g_ref[rows, :] = _silu(cols(hp, FOUR_W + 4 * HW, HW)).astype(BF16)

    n_groups = 5
    for s in range(len(sub_tiles)):
        if s == 0:
            start(0)
        for k in range(n_groups):
            column_group(s, k)
            if k == n_groups - 3 and s + 1 < len(sub_tiles):
                start(s + 1)


def _proj(x, mod, npre, lb_logits, win, *, layer, tiles_per_mod):
    t = x.shape[0]
    tm = TOKEN_TILE

    def tok(n):
        return pl.BlockSpec((tm, n), lambda i: (i, 0))

    shapes = [
        jax.ShapeDtypeStruct((t, FOUR_W), BF16),
        jax.ShapeDtypeStruct((t, HW), BF16),
        jax.ShapeDtypeStruct((t, HW), F32),
        jax.ShapeDtypeStruct((t, HW), F32),
        jax.ShapeDtypeStruct((t, HW), BF16),
        jax.ShapeDtypeStruct((t, HW), BF16),
    ]
    return pl.pallas_call(
        _proj_kernel,
        out_shape=shapes,
        grid=(t // tm,),
        in_specs=[
            tok(D_MODEL),
            pl.BlockSpec((1, N_MOD, D_MODEL), lambda i: (i // tiles_per_mod, 0, 0)),
            _const_spec((1, D_MODEL)),
            _const_spec((2, 2, HW)),
            _slab_spec((None, D_MODEL, MIX_COLS), (layer, 0, 0)),
        ],
        out_specs=[tok(FOUR_W), tok(HW), tok(HW), tok(HW), tok(HW), tok(HW)],
        compiler_params=_params("parallel"),
        name="mixer_in_proj",
    )(x, mod, npre, lb_logits, win)


def _cos_sin(n):
    k = np.arange(n)
    ang = 2.0 * np.pi * ((k[:, None] * k[None, :]) % n) / n
    return np.cos(ang), np.sin(ang)


def _bf16_const(a):
    return jnp.asarray(a, F32).astype(BF16)


def _channel_dft(u):
    def go(cs_ref):
        a_parts, b_parts = [], []
        for g in range(N_GROUPS):
            ab = _dot(u[:, g * GROUP:(g + 1) * GROUP], cs_ref[...])
            a_parts.append(ab[:, :GROUP])
            b_parts.append(ab[:, GROUP:])
        return (jnp.concatenate(a_parts, axis=1).astype(BF16),
                jnp.concatenate(b_parts, axis=1).astype(BF16))
    return go


def _fourier_seq_kernel(u_ref, cs_ref, pos_ref, y_ref, *, seq, n_seq):
    for s in range(n_seq):
        a, b = _channel_dft(u_ref[s * seq:(s + 1) * seq, :])(cs_ref)
        ab = jnp.concatenate([a, b], axis=0)
        y_ref[s * seq:(s + 1) * seq, :] = _dot(pos_ref[...], ab).astype(BF16)


def _fourier_seq(u, seq):
    t = u.shape[0]
    n_seq = 4
    cc, sc = _cos_sin(GROUP)
    cl, sl = _cos_sin(seq)
    cs = _bf16_const(np.concatenate([cc, sc], axis=1) / np.sqrt(GROUP))
    pos = _bf16_const(np.concatenate([cl, -sl], axis=1) / np.sqrt(seq))
    rows = n_seq * seq
    return pl.pallas_call(
        functools.partial(_fourier_seq_kernel, seq=seq, n_seq=n_seq),
        out_shape=jax.ShapeDtypeStruct((t, FOUR_W), BF16),
        grid=(t // rows,),
        in_specs=[
            pl.BlockSpec((rows, FOUR_W), lambda i: (i, 0)),
            _const_spec((GROUP, 2 * GROUP)),
            _const_spec((seq, 2 * seq)),
        ],
        out_specs=pl.BlockSpec((rows, FOUR_W), lambda i: (i, 0)),
        compiler_params=_params("parallel"),
        name="fourier_seq",
    )(u, cs, pos)


W_BLOCK = 8
CHANNEL_DFT_ROWS = 512


def _fourier_grid_kernel(u_ref, cs_ref, m1_ref, k2_ref, y_ref, ab_ref, p_ref, q_ref, *, n_rows):
    seq = n_rows * GRID_W
    for blk in range(seq // CHANNEL_DFT_ROWS):
        rows = slice(blk * CHANNEL_DFT_ROWS, (blk + 1) * CHANNEL_DFT_ROWS)
        a, b = _channel_dft(u_ref[rows, :])(cs_ref)
        ab_ref[0, rows, :] = a
        ab_ref[1, rows, :] = b

    def row_body(r, carry):
        start = pl.multiple_of(r * GRID_W, GRID_W)
        ab = jnp.concatenate([ab_ref[0, pl.ds(start, GRID_W), :], ab_ref[1, pl.ds(start, GRID_W), :]], axis=0)
        pq = _dot(m1_ref[...], ab)
        p_ref[r] = pq[:GRID_W]
        q_ref[r] = pq[GRID_W:]
        return carry

    lax.fori_loop(0, n_rows, row_body, 0, unroll=8)

    for j in range(GRID_W // W_BLOCK):
        ws = slice(j * W_BLOCK, (j + 1) * W_BLOCK)
        pb = p_ref[:, ws, :].reshape(n_rows * W_BLOCK, FOUR_W).astype(BF16)
        qb = q_ref[:, ws, :].reshape(n_rows * W_BLOCK, FOUR_W).astype(BF16)
        yb = _dot(k2_ref[...], jnp.concatenate([pb, qb], axis=0))
        p_ref[:, ws, :] = yb.reshape(n_rows, W_BLOCK, FOUR_W)
    for r0 in range(0, n_rows, SUBLANES):
        y_ref[0, r0:r0 + SUBLANES] = p_ref[r0:r0 + SUBLANES].astype(BF16)


def _fourier_grid(u, n_batch, n_rows):
    cc, sc = _cos_sin(GROUP)
    cw, sw = _cos_sin(GRID_W)
    cr, sr = _cos_sin(n_rows)
    cs = _bf16_const(np.concatenate([cc, sc], axis=1) / np.sqrt(GROUP))
    m1 = _bf16_const(np.block([[cw, -sw], [sw, cw]]) / np.sqrt(GRID_W))
    eye = np.eye(W_BLOCK)
    k2 = _bf16_const(np.concatenate([np.kron(cr, eye), -np.kron(sr, eye)], axis=1) / np.sqrt(n_rows))
    seq = n_rows * GRID_W
    nk = n_rows * W_BLOCK
    y = pl.pallas_call(
        functools.partial(_fourier_grid_kernel, n_rows=n_rows),
        out_shape=jax.ShapeDtypeStruct((n_batch, n_rows, GRID_W, FOUR_W), BF16),
        grid=(n_batch,),
        in_specs=[
            pl.BlockSpec((seq, FOUR_W), lambda b: (b, 0)),
            _const_spec((GROUP, 2 * GROUP)),
            _const_spec((2 * GRID_W, 2 * GRID_W)),
            _const_spec((nk, 2 * nk)),
        ],
        out_specs=pl.BlockSpec((1, n_rows, GRID_W, FOUR_W), lambda b: (b, 0, 0, 0)),
        scratch_shapes=[pltpu.VMEM((2, seq, FOUR_W), BF16),
                        pltpu.VMEM((n_rows, GRID_W, FOUR_W), F32),
                        pltpu.VMEM((n_rows, GRID_W, FOUR_W), F32)],
        compiler_params=_params("parallel"),
        name="fourier_grid",
    )(u, cs, m1, k2)
    return y.reshape(n_batch * seq, FOUR_W)


def _hgrn_kernel(*refs, reverse, zero_init, final, n_seq):
    it = iter(refs)
    q_ref, f_ref, v_ref = next(it), next(it), next(it)
    s0_ref = None if zero_init else next(it)
    if final:
        oprev_ref, sfwd_ref, gs_ref, gain_ref = next(it), next(it), next(it), next(it)
    o_ref, sfin_ref, st_ref = next(it), next(it), next(it)

    i = pl.program_id(1)
    c = HGRN_CHUNK
    nv = CHUNK_VREGS

    @pl.when(i == 0)
    def _():
        for sq in range(n_seq):
            for hd in range(N_HEADS):
                if zero_init:
                    st_ref[sq * N_HEADS + hd] = jnp.zeros((HEAD, HEAD), F32)
                else:
                    st_ref[sq * N_HEADS + hd] = s0_ref[sq, 0, 0, hd].T

    vreg_levels = [1 << b for b in range(nv.bit_length() - 1)]
    sub_levels = [1 << b for b in range(SUBLANES.bit_length() - 1)]
    sub = lax.broadcasted_iota(jnp.int32, (SUBLANES, HEAD), 0)
    out_bit = 0 if reverse else 1
    if final:
        unperm = _chunk_perm_matrix(transpose=True)

    lane = lax.broadcasted_iota(jnp.int32, (SUBLANES, c), 1)
    lane_i = lane & (SUBLANES - 1)
    lane_r = lane >> 3
    sub_c = lax.broadcasted_iota(jnp.int32, (SUBLANES, c), 0)
    same_i = lane_i == sub_c
    block_mask = {}
    for h in vreg_levels[1:]:
        for pair in range(nv // (2 * h)):
            key_lo = 2 * pair * h + (h if reverse else 0)
            block_mask[h, pair] = same_i & (lane_r >= key_lo) & (lane_r < key_lo + h)
    i_xor = lane_i ^ sub_c
    i_ordered = (sub_c < lane_i) if reverse else (sub_c > lane_i)
    sub_mask = {g: i_ordered & (i_xor >= g) & (i_xor < 2 * g) for g in sub_levels}
    same_pair = (lane_i >> 1) == (sub_c >> 1)
    intra_mask = [same_pair & (i_ordered | (same_i & ((lane_r >= x) if reverse else (lane_r <= x))))
                  for x in range(nv)]

    def vregs(x):
        return [x[SUBLANES * j:SUBLANES * (j + 1)] for j in range(nv)]

    def operand(xs):
        return jnp.concatenate(xs, axis=0).astype(BF16)

    def times(x, m):
        return m if x is None else x * m

    def lanes(hd):
        return slice(hd * HEAD, (hd + 1) * HEAD)

    def level_stage(sq, hd):
        sl = lanes(hd)
        f = f_ref[sq, :, sl]
        q = q_ref[sq, :, sl].astype(F32)
        vb = v_ref[sq, :, sl]
        k = 1.0 - f
        qv, kv = vregs(q), vregs(k)
        p = vregs(f)
        r = [None] * nv
        tot = vregs(f)
        a = [None] * nv
        adj = {}

        def merge(x, mask, scores):
            rows = scores[SUBLANES * x:SUBLANES * (x + 1)]
            a[x] = jnp.where(mask, rows, 0.0 if a[x] is None else a[x])

        for h in vreg_levels:
            is_out = [((x // h) & 1) == out_bit for x in range(nv)]
            if h == 1:
                for x in range(nv):
                    if is_out[x]:
                        adj[x] = jnp.sum(qv[x] * p[x] * kv[x ^ 1], axis=-1, keepdims=True)
            else:
                qop = operand([qv[x] * p[x] if is_out[x] else qv[x] for x in range(nv)])
                kop = operand([kv[x] if is_out[x] else times(r[x], kv[x]) for x in range(nv)])
                scores = _dot_nt(qop, kop)
                for x in range(nv):
                    if is_out[x]:
                        merge(x, block_mask[h, x // (2 * h)], scores)
            new_tot = []
            for b2 in range(len(tot) // 2):
                lo, hi = tot[2 * b2], tot[2 * b2 + 1]
                lo_rows = range(2 * b2 * h, (2 * b2 + 1) * h)
                hi_rows = range((2 * b2 + 1) * h, (2 * b2 + 2) * h)
                if reverse:
                    for x in lo_rows:
                        p[x] = p[x] * hi
                    for x in hi_rows:
                        r[x] = times(r[x], lo)
                else:
                    for x in hi_rows:
                        p[x] = p[x] * lo
                    for x in lo_rows:
                        r[x] = times(r[x], hi)
                new_tot.append(lo * hi)
            tot = new_tot
        t8 = tot[0]
        for g in sub_levels:
            qop = operand([qv[x] * p[x] for x in range(nv)])
            kop = operand([times(r[x], kv[x]) for x in range(nv)])
            scores = _dot_nt(qop, kop)
            for x in range(nv):
                merge(x, sub_mask[g], scores)
            hi_half = (sub & g) != 0
            w = jnp.where(hi_half, pltpu.roll(t8, g, 0), pltpu.roll(t8, SUBLANES - g, 0))
            grow_p = hi_half != reverse
            mp = jnp.where(grow_p, w, 1.0)
            mr = jnp.where(grow_p, 1.0, w)
            p = [x * mp for x in p]
            r = [times(x, mr) for x in r]
            t8 = t8 * w
        own = [jnp.sum(qv[x] * kv[x], axis=-1, keepdims=True) for x in range(nv)]
        qdec = operand([qv[x] * p[x] for x in range(nv)])
        kdec = operand([kv[x] * r[x] for x in range(nv)])
        return operand(a), qdec, kdec, t8[0:1, :], vb, own, adj

    def swap_sublanes(x, g):
        return jnp.where((sub & g) != 0, pltpu.roll(x, g, 0), pltpu.roll(x, SUBLANES - g, 0))

    def pair_totals(sq, hd):
        tot = vregs(f_ref[sq, :, lanes(hd)])
        while len(tot) > 1:
            tot = [tot[2 * b] * tot[2 * b + 1] for b in range(len(tot) // 2)]
        return tot[0] * swap_sublanes(tot[0], 1)

    def ratio_level_stage(sq, hd):
        sl = lanes(hd)
        f = f_ref[sq, :, sl]
        q = q_ref[sq, :, sl].astype(F32)
        vb = v_ref[sq, :, sl]
        k = 1.0 - f
        qv, kv = vregs(q), vregs(k)
        p = vregs(f)
        tot = vregs(f)
        for h in vreg_levels:
            new_tot = []
            for b2 in range(len(tot) // 2):
                lo, hi = tot[2 * b2], tot[2 * b2 + 1]
                if reverse:
                    for x in range(2 * b2 * h, (2 * b2 + 1) * h):
                        p[x] = p[x] * hi
                else:
                    for x in range((2 * b2 + 1) * h, (2 * b2 + 2) * h):
                        p[x] = p[x] * lo
                new_tot.append(lo * hi)
            tot = new_tot
        total = tot[0]

        def grow(g):
            w = swap_sublanes(total, g)
            grow_p = ((sub & g) != 0) != reverse
            return jnp.where(grow_p, w, 1.0), jnp.where(grow_p, 1.0, w), total * w

        mp, _, total = grow(1)
        p = [x * mp for x in p]
        qd = [qv[x] * p[x] for x in range(nv)]
        kd = [kv[x] / p[x] for x in range(nv)]
        a = [None] * nv

        def merge(masks, scores):
            for x in range(nv):
                rows = scores[SUBLANES * x:SUBLANES * (x + 1)]
                a[x] = jnp.where(masks[x], rows, 0.0 if a[x] is None else a[x])

        qop = operand(qd)
        merge(intra_mask, _dot_nt(qop, operand(kd)))
        kd = [x * total for x in kd]
        for g in sub_levels[1:]:
            merge([sub_mask[g]] * nv, _dot_nt(qop, operand(kd)))
            mp, mr, total = grow(g)
            qd = [x * mp for x in qd]
            kd = [x * mr for x in kd]
            qop = operand(qd)
        return operand(a), qop, operand(kd), total[0:1, :], vb, None, None

    def state_stage(sq, hd, a, qdec, kdec, total, vb, own, adj):
        slot = sq * N_HEADS + hd
        st = st_ref[slot]
        vf = vb.astype(F32)
        o = _dot(a, vb) + _dot_nt(qdec, st.astype(BF16))
        if own is not None:
            vv = vregs(vf)
            o = o + jnp.concatenate(
                [own[x] * vv[x] + (adj[x] * vv[x ^ 1] if x in adj else 0.0) for x in range(nv)], axis=0)
        vt = vf.T.astype(BF16)
        st_ref[slot] = st * total + _dot(vt, kdec)
        return o

    def output_stage(sq, hd, o):
        sl = lanes(hd)
        if final:
            o = o + oprev_ref[sq, :, sl]
            o = o * lax.rsqrt(jnp.mean(o * o, axis=-1, keepdims=True) + EPS)
            o = o * gain_ref[:, sl] * gs_ref[sq, :, sl].astype(F32)
            o_ref[sq, :, sl] = _dot(unperm, o.astype(BF16)).astype(BF16)
        else:
            o_ref[sq, :, sl] = o

    units = [(sq, hd) for sq in range(n_seq) for hd in range(N_HEADS)]

    def run(first_stage):
        scores, outs = {}, {}
        for step in range(len(units) + 2 * HGRN_STAGE_LAG):
            if step < len(units):
                scores[step] = first_stage(*units[step])
            j = step - HGRN_STAGE_LAG
            if 0 <= j < len(units):
                outs[j] = state_stage(*units[j], *scores.pop(j))
            j = step - 2 * HGRN_STAGE_LAG
            if 0 <= j < len(units):
                output_stage(*units[j], outs.pop(j))

    weakest = None
    for unit in units:
        t = pair_totals(*unit)
        weakest = t if weakest is None else jnp.minimum(weakest, t)
    safe = jnp.min(weakest) >= SAFE_BLOCK_DECAY

    @pl.when(safe)
    def _():
        run(ratio_level_stage)

    @pl.when(jnp.logical_not(safe))
    def _():
        run(level_stage)

    @pl.when(i == pl.num_programs(1) - 1)
    def _():
        for sq in range(n_seq):
            for hd in range(N_HEADS):
                if final:
                    sfin_ref[sq, 0, hd] = sfwd_ref[sq, hd]
                    sfin_ref[sq, 1, hd] = st_ref[sq * N_HEADS + hd].T
                else:
                    sfin_ref[sq, hd] = st_ref[sq * N_HEADS + hd].T


def _hgrn(q, f, v, *, n_batch, seq, reverse, s0=None, s0_dir=0, final_inputs=None):
    c = HGRN_CHUNK
    ns = HGRN_SEQS
    n_chunks = seq // c
    final = final_inputs is not None

    def tok_map(b, i):
        return (b, (n_chunks - 1 - i) if reverse else i, 0)

    tok = pl.BlockSpec((ns, c, HW), tok_map)
    in_specs = [tok, tok, tok]
    args = [q, f, v]
    if s0 is not None:
        in_specs.append(pl.BlockSpec((ns, 1, 1, N_HEADS, HEAD, HEAD), lambda b, i: (b, 0, s0_dir, 0, 0, 0)))
        args.append(s0)
    state_spec = pl.BlockSpec((ns, N_HEADS, HEAD, HEAD), lambda b, i: (b, 0, 0, 0))
    if final:
        oprev, sfwd, gs, gain = final_inputs
        in_specs += [tok, state_spec, tok, pl.BlockSpec((1, HW), lambda b, i: (0, 0))]
        args += [oprev, sfwd, gs, gain]
        out_shape = [jax.ShapeDtypeStruct((n_batch, seq, HW), BF16),
                     jax.ShapeDtypeStruct((n_batch, 2, N_HEADS, HEAD, HEAD), F32)]
        out_specs = [tok, pl.BlockSpec((ns, 2, N_HEADS, HEAD, HEAD), lambda b, i: (b, 0, 0, 0, 0))]
    else:
        out_shape = [jax.ShapeDtypeStruct((n_batch, seq, HW), F32),
                     jax.ShapeDtypeStruct((n_batch, N_HEADS, HEAD, HEAD), F32)]
        out_specs = [tok, state_spec]
    return pl.pallas_call(
        functools.partial(_hgrn_kernel, reverse=reverse, zero_init=s0 is None, final=final, n_seq=ns),
        out_shape=out_shape,
        grid=(n_batch // ns, n_chunks),
        in_specs=in_specs,
        out_specs=out_specs,
        scratch_shapes=[pltpu.VMEM((ns * N_HEADS, HEAD, HEAD), F32)],
        compiler_params=_params("arbitrary", "arbitrary"),
        name="hgrn2_bwd" if reverse else "hgrn2_fwd",
    )(*args)


def _merge_kernel(x_ref, y_ref, o_ref, mod_ref, npre_ref, npost_ref, wga_ref, wgb_ref, wf_ref, wh_ref, wo_ref,
                  out_ref):
    merged = []
    for rows in _sub_tiles():
        h = _mixer_input(x_ref[rows, :], mod_ref, npre_ref)
        merged.append((_sigmoid(_dot(h, wga_ref[...])) * _dot(y_ref[rows, :], wf_ref[...])
                       + _sigmoid(_dot(h, wgb_ref[...])) * _dot(o_ref[rows, :], wh_ref[...])).astype(BF16))
    for rows, mg in zip(_sub_tiles(), merged):
        m = _dot(mg, wo_ref[...])
        out_ref[rows, :] = x_ref[rows, :] + mod_ref[0, 5:6, :] * _rms(m, npost_ref[...])


def _merge(x, y, o, mod, npre, npost, win, wf, wh, wo, *, layer, tiles_per_mod):
    assert MIX_COLS % D_MODEL == 0
    gate_block = MIX_COLS // D_MODEL
    t = x.shape[0]
    tm = TOKEN_TILE

    def tok(n):
        return pl.BlockSpec((tm, n), lambda i: (i, 0))

    return pl.pallas_call(
        _merge_kernel,
        out_shape=jax.ShapeDtypeStruct((t, D_MODEL), F32),
        grid=(t // tm,),
        in_specs=[
            tok(D_MODEL), tok(FOUR_W), tok(HW),
            pl.BlockSpec((1, N_MOD, D_MODEL), lambda i: (i // tiles_per_mod, 0, 0)),
            _const_spec((1, D_MODEL)),
            _const_spec((1, D_MODEL)),
            _slab_spec((None, D_MODEL, D_MODEL), (layer, 0, gate_block)),
            _slab_spec((None, D_MODEL, D_MODEL), (layer, 0, gate_block + 1)),
            _slab_spec((None, FOUR_W, D_MODEL), (layer, 0, 0)),
            _slab_spec((None, HW, D_MODEL), (layer, 0, 0)),
            _slab_spec((None, D_MODEL, D_MODEL), (layer, 0, 0)),
        ],
        out_specs=tok(D_MODEL),
        compiler_params=_params("parallel"),
        name="mixer_merge",
    )(x, y, o, mod, npre, npost, win, win, wf, wh, wo)


def _trunk_layer(x, mod, w, *, layer, n_batch, seq, grid_rows, s0):
    tiles_per_mod = (x.shape[0] // TOKEN_TILE) // mod.shape[0]
    x1 = _ffn(x, mod, w["npre"][0], w["npost"][0], w["ffn_in"], w["ffn_out"],
              layer=layer, ffn_index=0, mod_base=0, tiles_per_mod=tiles_per_mod)
    u, q, ff, fb, v, gs = _proj(x1, mod, w["npre"][1], w["lb_logits"], w["w_in"],
                                layer=layer, tiles_per_mod=tiles_per_mod)
    if grid_rows is None:
        y = _fourier_seq(u, seq)
    else:
        y = _fourier_grid(u, n_batch, grid_rows)
    q, ff, fb, v, gs = (z.reshape(n_batch, seq, HW) for z in (q, ff, fb, v, gs))
    o_f, s_f = _hgrn(q, ff, v, n_batch=n_batch, seq=seq, reverse=False, s0=s0, s0_dir=0)
    o, states = _hgrn(q, fb, v, n_batch=n_batch, seq=seq, reverse=True, s0=s0, s0_dir=1,
                      final_inputs=(o_f, s_f, gs, w["hgrn_gain"]))
    x2 = _merge(x1, y, o.reshape(n_batch * seq, HW), mod, w["npre"][1], w["npost"][1], w["w_in"],
                w["w_four"], w["w_hgrn"], w["w_out"], layer=layer, tiles_per_mod=tiles_per_mod)
    out = _ffn(x2, mod, w["npre"][2], w["npost"][2], w["ffn_in"], w["ffn_out"],
               layer=layer, ffn_index=1, mod_base=6, tiles_per_mod=tiles_per_mod)
    return out, states


def kernel(x_prompt, x_sample, state_hgrn, c, c_ctx, w_mod, b_mod, norm_pre, norm_post, ffn_w_in, ffn_w_out,
           w_in, w_four, hgrn_gain, w_hgrn, w_out, lb_logits):
    batch, seq, d = x_prompt.shape
    dec_batch, dec_seq, _ = x_sample.shape
    depth = w_mod.shape[0]
    assert depth == 1 and d == D_MODEL
    rows = dec_seq // GRID_W
    layer = 0

    cond = jnp.concatenate([c_ctx[None, :], c, jnp.zeros((MOD_ROWS - 1 - dec_batch, d), F32)], axis=0)
    mod = _modulation(cond, w_mod[layer], b_mod[layer]).reshape(MOD_ROWS, N_MOD, d)
    mod_ctx = mod[0:1]
    mod_dec = mod[1:1 + dec_batch]

    w = dict(
        npre=norm_pre[layer].reshape(3, 1, d),
        npost=norm_post[layer].reshape(3, 1, d),
        ffn_in=ffn_w_in.astype(BF16),
        ffn_out=ffn_w_out.astype(BF16),
        w_in=w_in.astype(BF16),
        w_four=w_four.astype(BF16),
        w_hgrn=w_hgrn.astype(BF16),
        w_out=w_out.astype(BF16),
        hgrn_gain=hgrn_gain[layer].reshape(1, HW),
        lb_logits=lb_logits[layer:layer + 2],
    )

    y_prompt, states = _trunk_layer(x_prompt.reshape(batch * seq, d), mod_ctx, w, layer=layer,
                                    n_batch=batch, seq=seq, grid_rows=None, s0=None)
    new_state = states[:, None].astype(x_prompt.dtype)

    y_sample, _ = _trunk_layer(x_sample.reshape(dec_batch * dec_seq, d), mod_dec, w, layer=layer,
                               n_batch=dec_batch, seq=dec_seq, grid_rows=rows, s0=state_hgrn)
    return (y_prompt.reshape(batch, seq, d), y_sample.reshape(dec_batch, dec_seq, d), new_state)
```
